```python
import jax
import jax.numpy as jnp
from jax import lax
import numpy as np

D_MODEL = 1024
BATCH = 8
SEQ = 2048
DEPTH = 1
DEC_BATCH = 16
DEC_SEQ = 32
PAST_LEN = 1024

CHUNK = 64
A_HEADS = 8
A_HEAD_DIM = 64
A_WIDTH = A_HEADS * A_HEAD_DIM
BAND_CHUNKS = 8
MAX_REL = 128
B_HEADS = 4
B_HEAD_DIM = 128
B_WIDTH = B_HEADS * B_HEAD_DIM
CONV_W = 4
IN_WIDTH = 3 * A_WIDTH + 4 * B_WIDTH + 2 * B_HEADS + 2 * D_MODEL
MEM_TOKENS = 256
X_HEADS = 4
X_HEAD_DIM = D_MODEL // X_HEADS
PEER_HEADS = 8
N_KEYS = 128
N_EXPERTS = N_KEYS * N_KEYS
PEER_TOPK = 16
PEER_QDIM = 256
PEER_HALF = PEER_QDIM // 2
PEER_BLOCK = 128
EPS = 1e-6

kernel_name = 'hybrid_streaming_encoder_step'


def rms_norm(x, g):
    xf = x.astype(jnp.float32)
    r = lax.rsqrt(jnp.mean(xf * xf, axis=-1, keepdims=True) + EPS)
    return (xf * r).astype(x.dtype) * g.astype(x.dtype)


def _split_in(z):
    sizes = [A_WIDTH] * 3 + [B_WIDTH] * 4 + [2 * B_HEADS, D_MODEL, D_MODEL]
    bounds = [int(b) for b in np.cumsum(sizes)[:-1]]
    return jnp.split(z, bounds, axis=-1)


def _rel_bias(rel_bias, q_pos, k_pos):
    rel = jnp.clip(q_pos[..., :, None] - k_pos[..., None, :], -MAX_REL, MAX_REL) + MAX_REL
    return rel_bias[:, rel].astype(jnp.float32)


def band_attention_prompt(q, k, v, rel_bias):
    bsz, t, h, dh = q.shape
    nc = t // CHUNK
    width = (BAND_CHUNKS + 1) * CHUNK
    pad = BAND_CHUNKS * CHUNK
    kp = jnp.pad(k, ((0, 0), (pad, 0), (0, 0), (0, 0)))
    vp = jnp.pad(v, ((0, 0), (pad, 0), (0, 0), (0, 0)))
    band = jnp.arange(nc)[:, None] * CHUNK + jnp.arange(width)[None, :]
    kb = kp[:, band]
    vb = vp[:, band]
    q_pos = jnp.arange(nc)[:, None] * CHUNK + jnp.arange(CHUNK)[None, :]
    k_pos = band - pad
    s = jnp.einsum('bcqhd,bckhd->bhcqk', q.reshape(bsz, nc, CHUNK, h, dh), kb).astype(jnp.float32)
    s = s * (dh ** -0.5) + _rel_bias(rel_bias, q_pos, k_pos)[None]
    s = jnp.where((k_pos >= 0)[None, None, :, None, :], s, -jnp.inf)
    p = jax.nn.softmax(s, axis=-1).astype(v.dtype)
    o = jnp.einsum('bhcqk,bckhd->bcqhd', p, vb)
    return o.reshape(bsz, t, h * dh)


def band_attention_cached(q, k, v, k_prev, v_prev, rel_bias):
    bsz, s_len, h, dh = q.shape
    past = k_prev.shape[1]
    kk = jnp.concatenate([k_prev, k], axis=1)
    vv = jnp.concatenate([v_prev, v], axis=1)
    q_pos = past + jnp.arange(s_len)
    k_pos = jnp.arange(past + s_len)
    s = jnp.einsum('bqhd,bkhd->bhqk', q, kk).astype(jnp.float32)
    s = s * (dh ** -0.5) + _rel_bias(rel_bias, q_pos, k_pos)[None]
    p = jax.nn.softmax(s, axis=-1).astype(v.dtype)
    o = jnp.einsum('bhqk,bkhd->bqhd', p, vv)
    return o.reshape(bsz, s_len, h * dh)


def causal_conv(u, prev, w, b):
    t = u.shape[1]
    up = jnp.concatenate([prev, u], axis=1)
    out = b + sum(up[:, j:j + t] * w[j] for j in range(CONV_W))
    return out, up[:, up.shape[1] - (CONV_W - 1):]


def mlstm_chunk(C0, n0, m0, q, k, v, ig, lf):
    L = q.shape[1]
    b = jnp.cumsum(lf, axis=1)
    dm = b[:, :, None, :] - b[:, None, :, :] + ig[:, None, :, :]
    causal = jnp.tril(jnp.ones((L, L), dtype=bool))
    dm = jnp.where(causal[None, :, :, None], dm, -jnp.inf)
    inter = b + m0[:, None, :]
    m = jnp.maximum(dm.max(axis=2), inter)
    w_intra = jnp.exp(dm - m[:, :, None, :])
    w_inter = jnp.exp(inter - m)
    a = jnp.einsum('bthd,bshd->btsh', q, k) * w_intra
    num = jnp.einsum('btsh,bshd->bthd', a, v) + w_inter[..., None] * jnp.einsum('bthk,bhkv->bthv', q, C0)
    den = a.sum(axis=2) + w_inter * jnp.einsum('bthk,bhk->bth', q, n0)
    h = num / jnp.maximum(jnp.abs(den), jnp.exp(-m))[..., None]
    bl = b[:, -1]
    wk = bl[:, None, :] - b + ig
    ml = jnp.maximum(bl + m0, wk.max(axis=1))
    a0 = jnp.exp(bl + m0 - ml)
    ws = jnp.exp(wk - ml[:, None, :])
    C = a0[..., None, None] * C0 + jnp.einsum('bsh,bshk,bshv->bhkv', ws, k, v)
    n = a0[..., None] * n0 + jnp.einsum('bsh,bshk->bhk', ws, k)
    return h, (C, n, ml)


def mlstm_prompt(q, k, v, ig, lf):
    bsz, t, h, d = q.shape
    nc = t // CHUNK

    def to_blocks(a):
        return jnp.moveaxis(a.reshape((bsz, nc, CHUNK) + a.shape[2:]), 1, 0)

    init = (jnp.zeros((bsz, h, d, d), jnp.float32), jnp.zeros((bsz, h, d), jnp.float32),
            jnp.zeros((bsz, h), jnp.float32))

    def step(carry, xs):
        out, carry = mlstm_chunk(*carry, *xs)
        return carry, out

    carry, hs = lax.scan(step, init, (to_blocks(q), to_blocks(k), to_blocks(v), to_blocks(ig), to_blocks(lf)))
    return jnp.moveaxis(hs, 0, 1).reshape(bsz, t, h, d), carry


def memory_kv(mem, g_mem, w_mk, w_mv):
    bsz, m, _ = mem.shape
    mn = rms_norm(mem, g_mem)
    return ((mn @ w_mk).reshape(bsz, m, X_HEADS, X_HEAD_DIM),
            (mn @ w_mv).reshape(bsz, m, X_HEADS, X_HEAD_DIM))


def cross_attention(h, mem_k, mem_v, w_cq, w_co):
    bsz, t, _ = h.shape
    q = (h @ w_cq).reshape(bsz, t, X_HEADS, X_HEAD_DIM)
    s = jnp.einsum('bthd,bmhd->bhtm', q, mem_k).astype(jnp.float32) * (X_HEAD_DIM ** -0.5)
    p = jax.nn.softmax(s, axis=-1).astype(mem_v.dtype)
    o = jnp.einsum('bhtm,bmhd->bthd', p, mem_v).reshape(bsz, t, D_MODEL)
    return o @ w_co


def peer_ffn(h, w_pq, sub_keys, peer_u, peer_v):
    bsz, t, d = h.shape
    xf = h.reshape(bsz * t, d)
    n = xf.shape[0]
    nb = -(-n // PEER_BLOCK)
    xf = jnp.pad(xf, ((0, nb * PEER_BLOCK - n), (0, 0)))

    def block(xb):
        p = xb.shape[0]
        q = (xb @ w_pq).reshape(p, PEER_HEADS, 2, PEER_HALF)
        s = jnp.einsum('phcd,hckd->phck', q, sub_keys)
        sv, si = lax.top_k(s, PEER_TOPK)
        cand = sv[:, :, 0, :, None] + sv[:, :, 1, None, :]
        cidx = si[:, :, 0, :, None] * N_KEYS + si[:, :, 1, None, :]
        cv, ci = lax.top_k(cand.reshape(p, PEER_HEADS, PEER_TOPK * PEER_TOPK), PEER_TOPK)
        eidx = jnp.take_along_axis(cidx.reshape(p, PEER_HEADS, PEER_TOPK * PEER_TOPK), ci, axis=-1)
        g = jax.nn.softmax(cv.astype(jnp.float32), axis=-1).astype(xb.dtype)
        act = jax.nn.gelu(jnp.einsum('phkd,pd->phk', peer_u[eidx], xb), approximate=False)
        return jnp.einsum('phk,phkd->pd', g * act, peer_v[eidx])

    out = lax.map(block, xf.reshape(nb, PEER_BLOCK, d))
    return out.reshape(nb * PEER_BLOCK, d)[:n].reshape(bsz, t, d)


def _layer(x, mem_k, mem_v, a_k_prev, a_v_prev, conv_prev, C0, n0, m0,
           g_mix, w_in, conv_w, conv_b, b_if, g_head, rel_bias, w_a_up, w_b_up, w_out,
           g_cross, w_cq, w_co, g_ffn, w_pq, sub_keys, peer_u, peer_v):
    first = a_k_prev is None
    bsz, t, _ = x.shape
    f32 = jnp.float32
    h = rms_norm(x, g_mix)
    qa, ka, va, qb, kb, vb, ob, if_pre, ga, gb = _split_in(h @ w_in)
    qa = qa.reshape(bsz, t, A_HEADS, A_HEAD_DIM)
    ka = ka.reshape(bsz, t, A_HEADS, A_HEAD_DIM)
    va = va.reshape(bsz, t, A_HEADS, A_HEAD_DIM)
    if first:
        out_a = band_attention_prompt(qa, ka, va, rel_bias)
        keep = min(BAND_CHUNKS * CHUNK, t)
        new_ak, new_av = ka[:, t - keep:], va[:, t - keep:]
        conv_prev = jnp.zeros((bsz, CONV_W - 1, 2 * B_WIDTH), x.dtype)
    else:
        out_a = band_attention_cached(qa, ka, va, a_k_prev, a_v_prev, rel_bias)
        new_ak, new_av = ka, va
    qk_b, new_conv = causal_conv(jnp.concatenate([qb, kb], axis=-1), conv_prev, conv_w, conv_b)
    qb, kb = jnp.split(jax.nn.silu(qk_b), 2, axis=-1)
    qb = qb.reshape(bsz, t, B_HEADS, B_HEAD_DIM).astype(f32)
    kb = kb.reshape(bsz, t, B_HEADS, B_HEAD_DIM).astype(f32) * (B_HEAD_DIM ** -0.5)
    vb = vb.reshape(bsz, t, B_HEADS, B_HEAD_DIM).astype(f32)
    gates = (if_pre + b_if).astype(f32)
    ig = gates[..., :B_HEADS]
    lf = jax.nn.log_sigmoid(gates[..., B_HEADS:])
    if first:
        hb, (C, n, m) = mlstm_prompt(qb, kb, vb, ig, lf)
    else:
        hb, (C, n, m) = mlstm_chunk(C0.astype(f32), n0.astype(f32), m0.astype(f32), qb, kb, vb, ig, lf)
    hb = rms_norm(hb, g_head.reshape(B_HEADS, B_HEAD_DIM).astype(f32)).astype(x.dtype)
    hb = jax.nn.sigmoid(ob) * hb.reshape(bsz, t, B_WIDTH)
    mixed = jax.nn.sigmoid(ga) * (out_a @ w_a_up) + jax.nn.sigmoid(gb) * (hb @ w_b_up)
    x = x + mixed @ w_out
    x = x + cross_attention(rms_norm(x, g_cross), mem_k, mem_v, w_cq, w_co)
    x = x + peer_ffn(rms_norm(x, g_ffn), w_pq, sub_keys, peer_u, peer_v)
    return x, (new_ak, new_av, new_conv, C.astype(x.dtype), n.astype(x.dtype), m.astype(x.dtype))


def setup_inputs(seed: int = 0) -> dict:
    key = jax.random.key(seed)
    ks = iter(jax.random.split(key, 40))
    f32 = jnp.float32

    def nrm(shape, scale):
        return jax.random.normal(next(ks), shape, f32) * scale

    def gain(shape):
        return 1.0 + nrm(shape, 0.02)

    a_cache = min(BAND_CHUNKS * CHUNK, PAST_LEN)
    b_if = jnp.concatenate([nrm((DEPTH, B_HEADS), 0.1),
                            jnp.linspace(3.0, 6.0, B_HEADS, dtype=f32)[None] + nrm((DEPTH, B_HEADS), 0.1)], axis=-1)
    return {
        'x_prompt': nrm((BATCH, SEQ, D_MODEL), 1.0),
        'x_sample': nrm((DEC_BATCH, DEC_SEQ, D_MODEL), 1.0),
        'mem_prompt': nrm((BATCH, MEM_TOKENS, D_MODEL), 1.0),
        'cache_a_k': nrm((DEPTH, DEC_BATCH, a_cache, A_HEADS, A_HEAD_DIM), 1.0),
        'cache_a_v': nrm((DEPTH, DEC_BATCH, a_cache, A_HEADS, A_HEAD_DIM), 1.0),
        'state_b_conv': nrm((DEPTH, DEC_BATCH, CONV_W - 1, 2 * B_WIDTH), 1.0),
        'state_b_C': nrm((DEPTH, DEC_BATCH, B_HEADS, B_HEAD_DIM, B_HEAD_DIM), 0.1),
        'state_b_n': nrm((DEPTH, DEC_BATCH, B_HEADS, B_HEAD_DIM), 0.1),
        'state_b_m': nrm((DEPTH, DEC_BATCH, B_HEADS), 0.5),
        'cache_mem_k': nrm((DEPTH, DEC_BATCH, MEM_TOKENS, X_HEADS, X_HEAD_DIM), 1.0),
        'cache_mem_v': nrm((DEPTH, DEC_BATCH, MEM_TOKENS, X_HEADS, X_HEAD_DIM), 1.0),
        'g_mix': gain((DEPTH, D_MODEL)),
        'w_in': nrm((DEPTH, D_MODEL, IN_WIDTH), D_MODEL ** -0.5),
        'conv_w': nrm((DEPTH, CONV_W, 2 * B_WIDTH), 0.5),
        'conv_b': nrm((DEPTH, 2 * B_WIDTH), 0.02),
        'b_if': b_if,
        'g_head': gain((DEPTH, B_WIDTH)),
        'rel_bias': nrm((DEPTH, A_HEADS, 2 * MAX_REL + 1), 0.1),
        'w_a_up': nrm((DEPTH, A_WIDTH, D_MODEL), A_WIDTH ** -0.5),
        'w_b_up': nrm((DEPTH, B_WIDTH, D_MODEL), B_WIDTH ** -0.5),
        'w_out': nrm((DEPTH, D_MODEL, D_MODEL), D_MODEL ** -0.5),
        'g_mem': gain((DEPTH, D_MODEL)),
        'w_mk': nrm((DEPTH, D_MODEL, D_MODEL), D_MODEL ** -0.5),
        'w_mv': nrm((DEPTH, D_MODEL, D_MODEL), D_MODEL ** -0.5),
        'g_cross': gain((DEPTH, D_MODEL)),
        'w_cq': nrm((DEPTH, D_MODEL, D_MODEL), D_MODEL ** -0.5),
        'w_co': nrm((DEPTH, D_MODEL, D_MODEL), D_MODEL ** -0.5),
        'g_ffn': gain((DEPTH, D_MODEL)),
        'w_pq': nrm((DEPTH, D_MODEL, PEER_HEADS * PEER_QDIM), D_MODEL ** -0.5),
        'sub_keys': nrm((DEPTH, PEER_HEADS, 2, N_KEYS, PEER_HALF), PEER_HALF ** -0.5),
        'peer_u': nrm((DEPTH, N_EXPERTS, D_MODEL), D_MODEL ** -0.5),
        'peer_v': nrm((DEPTH, N_EXPERTS, D_MODEL), 0.1),
        'g_final': gain((D_MODEL,)),
    }


def _stack(states, i):
    return jnp.stack([st[i] for st in states])


def reference(x_prompt, x_sample, mem_prompt, cache_a_k, cache_a_v, state_b_conv, state_b_C, state_b_n,
              state_b_m, cache_mem_k, cache_mem_v, g_mix, w_in, conv_w, conv_b, b_if, g_head, rel_bias,
              w_a_up, w_b_up, w_out, g_mem, w_mk, w_mv, g_cross, w_cq, w_co, g_ffn, w_pq, sub_keys,
              peer_u, peer_v, g_final):
    yp, ys = x_prompt, x_sample
    ps, ss = [], []
    for l in range(DEPTH):
        wl = (g_mix[l], w_in[l], conv_w[l], conv_b[l], b_if[l], g_head[l], rel_bias[l], w_a_up[l], w_b_up[l],
              w_out[l], g_cross[l], w_cq[l], w_co[l], g_ffn[l], w_pq[l], sub_keys[l], peer_u[l], peer_v[l])
        mk_p, mv_p = memory_kv(mem_prompt, g_mem[l], w_mk[l], w_mv[l])
        yp, sp = _layer(yp, mk_p, mv_p, None, None, None, None, None, None, *wl)
        ys, sq = _layer(ys, cache_mem_k[l], cache_mem_v[l], cache_a_k[l], cache_a_v[l], state_b_conv[l],
                        state_b_C[l], state_b_n[l], state_b_m[l], *wl)
        ps.append(sp + (mk_p, mv_p))
        ss.append(sq)
    y_prompt = rms_norm(yp, g_final)
    y_sample = rms_norm(ys, g_final)
    return (y_prompt, y_sample,
            _stack(ps, 0), _stack(ps, 1), _stack(ps, 2), _stack(ps, 3), _stack(ps, 4), _stack(ps, 5),
            _stack(ps, 6), _stack(ps, 7),
            _stack(ss, 0), _stack(ss, 1), _stack(ss, 2), _stack(ss, 3), _stack(ss, 4), _stack(ss, 5))
```

```python
import functools

import numpy as np
import jax
import jax.numpy as jnp
from jax import lax
from jax.experimental import pallas as pl
from jax.experimental.pallas import tpu as pltpu

F32 = jnp.float32
BF16 = jnp.bfloat16

D_MODEL = 1024
CHUNK = 64
A_HEADS = 8
A_HEAD_DIM = 64
A_WIDTH = A_HEADS * A_HEAD_DIM
BAND_CHUNKS = 8
MAX_REL = 128
B_HEADS = 4
B_HEAD_DIM = 128
B_WIDTH = B_HEADS * B_HEAD_DIM
CONV_W = 4
MEM_TOKENS = 256
X_HEADS = 4
X_HEAD_DIM = D_MODEL // X_HEADS
PEER_HEADS = 8
N_KEYS = 128
PEER_TOPK = 16
PEER_QDIM = 256
PEER_HALF = PEER_QDIM // 2
EPS = 1e-6

LANES = 128
SUBLANES = 8
VMEM_LIMIT = 56 * 1024 * 1024
NEG_BIG = -1e30

A_PAIRS = A_HEADS // 2
IF_PAD = LANES
PEER_TOK = 512
PEER_EBLK = 512
PEER_IBLK = PEER_EBLK // N_KEYS


def _cparams(*sem):
    return pltpu.CompilerParams(dimension_semantics=sem, vmem_limit_bytes=VMEM_LIMIT)


def _rms(xf, g):
    r = lax.rsqrt(jnp.mean(xf * xf, axis=-1, keepdims=True) + EPS)
    return xf * r * g


def _sigmoid(x):
    return 1.0 / (1.0 + jnp.exp(-x))


def _log_sigmoid(x):
    return jnp.minimum(x, 0.0) - jnp.log1p(jnp.exp(-jnp.abs(x)))


def _norm_proj_kernel(x_ref, g_ref, *refs):
    n = len(refs) // 2
    h = _rms(x_ref[...], g_ref[...]).astype(BF16)
    for w_ref, o_ref in zip(refs[:n], refs[n:]):
        o_ref[...] = jnp.dot(h, w_ref[...], preferred_element_type=F32)


def _norm_proj(x, g, ws, tile):
    n_tok, d = x.shape
    assert n_tok % tile == 0
    in_specs = [pl.BlockSpec((tile, d), lambda i: (i, 0)), pl.BlockSpec((1, d), lambda i: (0, 0))]
    in_specs += [pl.BlockSpec(w.shape, lambda i: (0, 0)) for w in ws]
    out_specs = [pl.BlockSpec((tile, w.shape[1]), lambda i: (i, 0)) for w in ws]
    out_shape = [jax.ShapeDtypeStruct((n_tok, w.shape[1]), F32) for w in ws]
    return pl.pallas_call(
        _norm_proj_kernel, grid=(n_tok // tile,), in_specs=in_specs, out_specs=out_specs, out_shape=out_shape,
        compiler_params=_cparams("parallel"), name="norm_proj")(x, g.reshape(1, d), *ws)


def _attn_pair(qp, kb, vb, bias, valid_from):
    m_rows = qp.shape[0]
    lane = lax.broadcasted_iota(jnp.int32, qp.shape, 1)
    q0 = jnp.where(lane < A_HEAD_DIM, qp, 0.0)
    q1 = jnp.where(lane >= A_HEAD_DIM, qp, 0.0)
    qs = jnp.concatenate([q0, q1], axis=0).astype(BF16)
    s = lax.dot_general(qs, kb, (((1,), (1,)), ((), ())), preferred_element_type=F32) + bias
    if valid_from is not None:
        col = lax.broadcasted_iota(jnp.int32, s.shape, 1)
        s = jnp.where(col >= valid_from, s, NEG_BIG)
    mx = jnp.max(s, axis=-1, keepdims=True)
    p = jnp.exp(s - mx)
    den = jnp.sum(p, axis=-1, keepdims=True)
    o = jnp.dot(p.astype(BF16), vb, preferred_element_type=F32) / den
    return jnp.where(lane < A_HEAD_DIM, o[:m_rows], o[m_rows:])


def _band_prompt_kernel(q_ref, k_ref, v_ref, bias_ref, o_ref, kpad, vpad):
    c = pl.program_id(1)
    pad = BAND_CHUNKS * CHUNK
    width = pad + CHUNK

    @pl.when(c == 0)
    def _():
        kpad[0:pad, :] = jnp.zeros((pad, A_WIDTH), BF16)
        vpad[0:pad, :] = jnp.zeros((pad, A_WIDTH), BF16)
        kpad[pad:, :] = k_ref[...].astype(BF16)
        vpad[pad:, :] = v_ref[...].astype(BF16)

    start = pl.multiple_of(c * CHUNK, CHUNK)
    valid_from = pad - c * CHUNK
    for hp in range(A_PAIRS):
        cols = slice(hp * LANES, (hp + 1) * LANES)
        qp = q_ref[:, cols] * (A_HEAD_DIM ** -0.5)
        kb = kpad[pl.ds(start, width), cols]
        vb = vpad[pl.ds(start, width), cols]
        o_ref[:, cols] = _attn_pair(qp, kb, vb, bias_ref[hp], valid_from)


def _band_prompt(za, bias, bsz, t):
    nc = t // CHUNK
    pad = BAND_CHUNKS * CHUNK
    return pl.pallas_call(
        _band_prompt_kernel, grid=(bsz, nc),
        in_specs=[pl.BlockSpec((CHUNK, A_WIDTH), lambda b, c: (b * nc + c, 0)),
                  pl.BlockSpec((t, A_WIDTH), lambda b, c: (b, 1)),
                  pl.BlockSpec((t, A_WIDTH), lambda b, c: (b, 2)),
                  pl.BlockSpec(bias.shape, lambda b, c: (0, 0, 0))],
        out_specs=pl.BlockSpec((CHUNK, A_WIDTH), lambda b, c: (b * nc + c, 0)),
        out_shape=jax.ShapeDtypeStruct((bsz * t, A_WIDTH), F32),
        scratch_shapes=[pltpu.VMEM((pad + t, A_WIDTH), BF16), pltpu.VMEM((pad + t, A_WIDTH), BF16)],
        compiler_params=_cparams("parallel", "arbitrary"), name="band_prompt")(za, za, za, bias)


def _band_cached_kernel(q_ref, k_ref, v_ref, bias_ref, o_ref):
    for hp in range(A_PAIRS):
        cols = slice(hp * LANES, (hp + 1) * LANES)
        qp = q_ref[:, cols] * (A_HEAD_DIM ** -0.5)
        kb = k_ref[:, cols].astype(BF16)
        vb = v_ref[:, cols].astype(BF16)
        o_ref[:, cols] = _attn_pair(qp, kb, vb, bias_ref[hp], None)


def _band_cached(za, k_all, v_all, bias, bsz, s_len):
    lk = k_all.shape[1]
    return pl.pallas_call(
        _band_cached_kernel, grid=(bsz,),
        in_specs=[pl.BlockSpec((s_len, A_WIDTH), lambda b: (b, 0)),
                  pl.BlockSpec((None, lk, A_WIDTH), lambda b: (b, 0, 0)),
                  pl.BlockSpec((None, lk, A_WIDTH), lambda b: (b, 0, 0)),
                  pl.BlockSpec(bias.shape, lambda b: (0, 0, 0))],
        out_specs=pl.BlockSpec((s_len, A_WIDTH), lambda b: (b, 0)),
        out_shape=jax.ShapeDtypeStruct((bsz * s_len, A_WIDTH), F32),
        compiler_params=_cparams("parallel"), name="band_cached")(za, k_all, v_all, bias)


def _pair_bias(rel_bias, q_pos, k_pos):
    rel = np.clip(q_pos[:, None] - k_pos[None, :], -MAX_REL, MAX_REL) + MAX_REL
    full = rel_bias[:, rel]
    return full.reshape(A_PAIRS, 2 * q_pos.shape[0], k_pos.shape[0]).astype(F32)


def _mlstm_kernel(zb_ref, zif_ref, convw_ref, convb_ref, bif_ref, ghead_ref, cprev_ref, c0_ref, n0_ref, m0_ref,
                  hb_ref, c_out, n_out, m_out, c_s, n_s, m_s, ubuf):
    c = pl.program_id(1)
    nchunks = pl.num_programs(1)
    L = zif_ref.shape[0]
    qk_w = 2 * B_WIDTH
    hist = SUBLANES

    @pl.when(c == 0)
    def _():
        c_s[...] = c0_ref[...]
        n_s[...] = n0_ref[...]
        m_s[...] = m0_ref[...]
        ubuf[0:hist, :] = cprev_ref[...]

    ubuf[hist:hist + L, :] = zb_ref[:, 0:qk_w]
    acc = jnp.broadcast_to(convb_ref[...], (L, qk_w))
    for j in range(CONV_W):
        off = hist - (CONV_W - 1) + j
        acc = acc + ubuf[off:off + L, :] * convw_ref[j:j + 1, :]
    ubuf[0:hist, :] = ubuf[L:L + hist, :]
    qk = acc * _sigmoid(acc)

    gates = zif_ref[...] + bif_ref[...]
    eye = (lax.broadcasted_iota(jnp.int32, (LANES, LANES), 0) ==
           lax.broadcasted_iota(jnp.int32, (LANES, LANES), 1)).astype(F32)
    gates_t = lax.dot_general(eye, gates, (((1,), (1,)), ((), ())), precision=lax.Precision.HIGHEST,
                              preferred_element_type=F32)
    row = lax.broadcasted_iota(jnp.int32, (L, L), 0)
    colm = lax.broadcasted_iota(jnp.int32, (L, L), 1)
    causal = colm <= row
    lower = causal.astype(F32)
    upper = (row <= colm).astype(F32)
    b_col_all = jnp.dot(lower, _log_sigmoid(gates), precision=lax.Precision.HIGHEST, preferred_element_type=F32)
    b_row_all = jnp.dot(_log_sigmoid(gates_t), upper, precision=lax.Precision.HIGHEST, preferred_element_type=F32)

    for h in range(B_HEADS):
        cols = slice(h * B_HEAD_DIM, (h + 1) * B_HEAD_DIM)
        qh = qk[:, h * B_HEAD_DIM:(h + 1) * B_HEAD_DIM]
        kh = qk[:, B_WIDTH + h * B_HEAD_DIM:B_WIDTH + (h + 1) * B_HEAD_DIM] * (B_HEAD_DIM ** -0.5)
        vh = zb_ref[:, qk_w + h * B_HEAD_DIM:qk_w + (h + 1) * B_HEAD_DIM]
        oh = zb_ref[:, qk_w + B_WIDTH + h * B_HEAD_DIM:qk_w + B_WIDTH + (h + 1) * B_HEAD_DIM]
        qb, kb, vb = qh.astype(BF16), kh.astype(BF16), vh.astype(BF16)

        ig_col = gates[:, h:h + 1]
        ig_row = gates_t[h:h + 1, :]
        b_col = b_col_all[:, B_HEADS + h:B_HEADS + h + 1]
        b_row = b_row_all[B_HEADS + h:B_HEADS + h + 1, :]
        m0 = m_s[h:h + 1, 0:1]
        c0 = c_s[h]
        n0 = n_s[h:h + 1, :]

        dm = jnp.where(causal, b_col - b_row + ig_row, NEG_BIG)
        inter = b_col + m0
        m = jnp.maximum(jnp.max(dm, axis=-1, keepdims=True), inter)
        w_intra = jnp.exp(dm - m)
        w_inter = jnp.exp(inter - m)
        a = lax.dot_general(qb, kb, (((1,), (1,)), ((), ())), preferred_element_type=F32) * w_intra
        num = (jnp.dot(a.astype(BF16), vb, preferred_element_type=F32) +
               w_inter * jnp.dot(qb, c0.astype(BF16), preferred_element_type=F32))
        den = jnp.sum(a, axis=-1, keepdims=True) + w_inter * jnp.sum(qh * n0, axis=-1, keepdims=True)
        hh = num / jnp.maximum(jnp.abs(den), jnp.exp(-m))

        bl = b_col[L - 1:L, :]
        wk = bl - b_col + ig_col
        ml = jnp.maximum(bl + m0, jnp.max(wk, axis=0, keepdims=True))
        a0 = jnp.exp(bl + m0 - ml)
        ws = jnp.exp(wk - ml)
        kw = kh * ws
        c_s[h] = a0 * c0 + lax.dot_general(kw.astype(BF16), vb, (((0,), (0,)), ((), ())), preferred_element_type=F32)
        n_s[h:h + 1, :] = a0 * n0 + jnp.sum(kw, axis=0, keepdims=True)
        m_s[h:h + 1, :] = jnp.broadcast_to(ml, (1, LANES))

        hn = _rms(hh, ghead_ref[:, cols])
        hb_ref[:, cols] = _sigmoid(oh) * hn

    @pl.when(c == nchunks - 1)
    def _():
        c_out[...] = c_s[...]
        n_out[...] = n_s[...]
        m_out[...] = m_s[...]


def _mlstm(zb, zif, conv_w, conv_b, bif, g_head, conv_prev, c0, n0, m0, bsz, t, L):
    nc = t // L
    qk_w = 2 * B_WIDTH
    hd = B_HEAD_DIM
    full2 = lambda b, c: (0, 0)
    outs = pl.pallas_call(
        _mlstm_kernel, grid=(bsz, nc),
        in_specs=[pl.BlockSpec((L, zb.shape[1]), lambda b, c: (b * nc + c, 0)),
                  pl.BlockSpec((L, IF_PAD), lambda b, c: (b * nc + c, 0)),
                  pl.BlockSpec((CONV_W, qk_w), full2),
                  pl.BlockSpec((1, qk_w), full2),
                  pl.BlockSpec((1, IF_PAD), full2),
                  pl.BlockSpec((1, B_WIDTH), full2),
                  pl.BlockSpec((None, SUBLANES, qk_w), lambda b, c: (b, 0, 0)),
                  pl.BlockSpec((None, B_HEADS, hd, hd), lambda b, c: (b, 0, 0, 0)),
                  pl.BlockSpec((None, SUBLANES, hd), lambda b, c: (b, 0, 0)),
                  pl.BlockSpec((None, SUBLANES, LANES), lambda b, c: (b, 0, 0))],
        out_specs=[pl.BlockSpec((L, B_WIDTH), lambda b, c: (b * nc + c, 0)),
                   pl.BlockSpec((None, B_HEADS, hd, hd), lambda b, c: (b, 0, 0, 0)),
                   pl.BlockSpec((None, SUBLANES, hd), lambda b, c: (b, 0, 0)),
                   pl.BlockSpec((None, SUBLANES, LANES), lambda b, c: (b, 0, 0))],
        out_shape=[jax.ShapeDtypeStruct((bsz * t, B_WIDTH), F32),
                   jax.ShapeDtypeStruct((bsz, B_HEADS, hd, hd), F32),
                   jax.ShapeDtypeStruct((bsz, SUBLANES, hd), F32),
                   jax.ShapeDtypeStruct((bsz, SUBLANES, LANES), F32)],
        scratch_shapes=[pltpu.VMEM((B_HEADS, hd, hd), F32), pltpu.VMEM((SUBLANES, hd), F32),
                        pltpu.VMEM((SUBLANES, LANES), F32), pltpu.VMEM((SUBLANES + L, qk_w), F32)],
        compiler_params=_cparams("parallel", "arbitrary"), name="mlstm")(
            zb, zif, conv_w, conv_b, bif, g_head, conv_prev, c0, n0, m0)
    return outs


def _merge_kernel(x_ref, oa_ref, hb_ref, zg_ref, wa_ref, wb_ref, wo_ref, o_ref):
    ua = jnp.dot(oa_ref[...].astype(BF16), wa_ref[...], preferred_element_type=F32)
    ub = jnp.dot(hb_ref[...].astype(BF16), wb_ref[...], preferred_element_type=F32)
    mixed = _sigmoid(zg_ref[:, 0:D_MODEL]) * ua + _sigmoid(zg_ref[:, D_MODEL:]) * ub
    o_ref[...] = x_ref[...] + jnp.dot(mixed.astype(BF16), wo_ref[...], preferred_element_type=F32)


def _merge(x, oa, hb, zg, wa, wb, wo, tile):
    n_tok, d = x.shape
    row = lambda i: (i, 0)
    full = lambda i: (0, 0)
    return pl.pallas_call(
        _merge_kernel, grid=(n_tok // tile,),
        in_specs=[pl.BlockSpec((tile, d), row), pl.BlockSpec((tile, A_WIDTH), row), pl.BlockSpec((tile, B_WIDTH), row),
                  pl.BlockSpec((tile, 2 * d), row), pl.BlockSpec(wa.shape, full), pl.BlockSpec(wb.shape, full),
                  pl.BlockSpec(wo.shape, full)],
        out_specs=pl.BlockSpec((tile, d), row), out_shape=jax.ShapeDtypeStruct((n_tok, d), F32),
        compiler_params=_cparams("parallel"), name="merge")(x, oa, hb, zg, wa, wb, wo)


def _cross_kernel(x_ref, g_ref, k_ref, v_ref, wq_ref, wo_ref, o_ref):
    x = x_ref[...]
    hn = _rms(x, g_ref[...]).astype(BF16)
    q = jnp.dot(hn, wq_ref[...], preferred_element_type=F32) * (X_HEAD_DIM ** -0.5)
    outs = []
    for h in range(X_HEADS):
        cols = slice(h * X_HEAD_DIM, (h + 1) * X_HEAD_DIM)
        qh = q[:, cols].astype(BF16)
        kh = k_ref[:, cols].astype(BF16)
        vh = v_ref[:, cols].astype(BF16)
        s = lax.dot_general(qh, kh, (((1,), (1,)), ((), ())), preferred_element_type=F32)
        mx = jnp.max(s, axis=-1, keepdims=True)
        p = jnp.exp(s - mx)
        den = jnp.sum(p, axis=-1, keepdims=True)
        outs.append((jnp.dot(p.astype(BF16), vh, preferred_element_type=F32) / den).astype(BF16))
    o = jnp.concatenate(outs, axis=-1)
    o_ref[...] = x + jnp.dot(o, wo_ref[...], preferred_element_type=F32)


def _cross(x, g, mem_k, mem_v, wq, wo, tile, tiles_per_batch):
    n_tok, d = x.shape
    row = lambda i: (i, 0)
    full = lambda i: (0, 0)
    mem = lambda i: (i // tiles_per_batch, 0, 0)
    return pl.pallas_call(
        _cross_kernel, grid=(n_tok // tile,),
        in_specs=[pl.BlockSpec((tile, d), row), pl.BlockSpec((1, d), full),
                  pl.BlockSpec((None, MEM_TOKENS, d), mem), pl.BlockSpec((None, MEM_TOKENS, d), mem),
                  pl.BlockSpec(wq.shape, full), pl.BlockSpec(wo.shape, full)],
        out_specs=pl.BlockSpec((tile, d), row), out_shape=jax.ShapeDtypeStruct((n_tok, d), F32),
        compiler_params=_cparams("parallel"), name="cross")(x, g.reshape(1, d), mem_k, mem_v, wq, wo)


def _peer_route_kernel(x_ref, g_ref, wpq_ref, sk_ref, ht_ref, s_ref, w_ref, tau_ref, qt_s, sv_s):
    P = x_ref.shape[0]
    hn = _rms(x_ref[...], g_ref[...])
    ht = hn.T.astype(BF16)
    ht_ref[...] = ht
    qt_s[...] = jnp.dot(wpq_ref[...], ht, preferred_element_type=F32)

    def score_body(hc, carry):
        off = pl.multiple_of(hc * PEER_HALF, PEER_HALF)
        qt = qt_s[pl.ds(off, PEER_HALF), :].astype(BF16)
        st = jnp.dot(sk_ref[hc], qt, preferred_element_type=F32)
        s_ref[hc] = st
        work = st
        tops = []
        for _ in range(PEER_TOPK):
            mx = jnp.max(work, axis=0, keepdims=True)
            tops.append(mx)
            work = jnp.where(work == mx, NEG_BIG, work)
        sv_s[hc] = jnp.concatenate(tops, axis=0)
        return carry

    lax.fori_loop(0, 2 * PEER_HEADS, score_body, 0)

    def head_body(h, carry):
        sv0 = sv_s[2 * h]
        sv1 = sv_s[2 * h + 1]
        half = PEER_TOPK // 2
        pieces = [sv0[0:1] + sv1, sv0[1:2] + sv1[0:half]]
        pieces += [sv0[a:a + 1] + sv1[0:half] for a in range(2, half)]
        pieces += [sv0[half:] + sv1[0:1]]
        cand = jnp.concatenate(pieces, axis=0)
        top = sv0[0:1] + sv1[0:1]
        z = jnp.zeros_like(top)
        mx = top
        for _ in range(PEER_TOPK):
            mx = jnp.max(cand, axis=0, keepdims=True)
            z = z + jnp.exp(mx - top)
            cand = jnp.where(cand == mx, NEG_BIG, cand)
        tau_ref[pl.ds(h, 1), :] = mx
        w_ref[2 * h] = jnp.exp(s_ref[2 * h] - sv0[0:1])
        w_ref[2 * h + 1] = jnp.exp(s_ref[2 * h + 1] - sv1[0:1]) / z
        return carry

    lax.fori_loop(0, PEER_HEADS, head_body, 0)


def _peer_route(x, g, wpq_t, sk, tile):
    n_tok, d = x.shape
    nhc = 2 * PEER_HEADS
    col3 = lambda i: (0, 0, i)
    return pl.pallas_call(
        _peer_route_kernel, grid=(n_tok // tile,),
        in_specs=[pl.BlockSpec((tile, d), lambda i: (i, 0)), pl.BlockSpec((1, d), lambda i: (0, 0)),
                  pl.BlockSpec(wpq_t.shape, lambda i: (0, 0)), pl.BlockSpec(sk.shape, lambda i: (0, 0, 0))],
        out_specs=[pl.BlockSpec((d, tile), lambda i: (0, i)),
                   pl.BlockSpec((nhc, N_KEYS, tile), col3), pl.BlockSpec((nhc, N_KEYS, tile), col3),
                   pl.BlockSpec((PEER_HEADS, tile), lambda i: (0, i))],
        out_shape=[jax.ShapeDtypeStruct((d, n_tok), BF16),
                   jax.ShapeDtypeStruct((nhc, N_KEYS, n_tok), F32), jax.ShapeDtypeStruct((nhc, N_KEYS, n_tok), F32),
                   jax.ShapeDtypeStruct((PEER_HEADS, n_tok), F32)],
        scratch_shapes=[pltpu.VMEM((PEER_HEADS * PEER_QDIM, tile), F32), pltpu.VMEM((nhc, PEER_TOPK, tile), F32)],
        compiler_params=_cparams("parallel"), name="peer_route")(x, g.reshape(1, d), wpq_t, sk)


def _gelu(z):
    return 0.5 * z * (1.0 + lax.erf(z * (2.0 ** -0.5)))


def _peer_dense_kernel(x_ref, ht_ref, s_ref, w_ref, tau_ref, u_ref, vt_ref, gf_ref, y_ref, acc_s, c_s):
    e = pl.program_id(1)
    ne = pl.num_programs(1)
    P = x_ref.shape[0]

    @pl.when(e == 0)
    def _():
        acc_s[...] = jnp.zeros_like(acc_s)

    act_t = _gelu(jnp.dot(u_ref[...], ht_ref[...], preferred_element_type=F32))
    for ii in range(PEER_IBLK):
        i = e * PEER_IBLK + ii
        g = jnp.zeros((N_KEYS, P), F32)
        for h in range(PEER_HEADS):
            s0_i = s_ref[2 * h, pl.ds(i, 1), :]
            w0_i = w_ref[2 * h, pl.ds(i, 1), :]
            cand = s_ref[2 * h + 1] + s0_i
            g = g + jnp.where(cand >= tau_ref[h:h + 1, :], w_ref[2 * h + 1], 0.0) * w0_i
        c_s[ii * N_KEYS:(ii + 1) * N_KEYS, :] = (g * act_t[ii * N_KEYS:(ii + 1) * N_KEYS, :]).astype(BF16)
    acc_s[...] += jnp.dot(vt_ref[...], c_s[...], preferred_element_type=F32)

    @pl.when(e == ne - 1)
    def _():
        y = x_ref[...] + acc_s[...].T
        y_ref[...] = _rms(y, gf_ref[...])


def _peer_dense(x, ht, s_t, w_t, tau, u, vt, g_final, tile):
    n_tok, d = x.shape
    n_exp = u.shape[0]
    nhc = 2 * PEER_HEADS
    return pl.pallas_call(
        _peer_dense_kernel, grid=(n_tok // tile, n_exp // PEER_EBLK),
        in_specs=[pl.BlockSpec((tile, d), lambda t, e: (t, 0)),
                  pl.BlockSpec((d, tile), lambda t, e: (0, t)),
                  pl.BlockSpec((nhc, N_KEYS, tile), lambda t, e: (0, 0, t)),
                  pl.BlockSpec((nhc, N_KEYS, tile), lambda t, e: (0, 0, t)),
                  pl.BlockSpec((PEER_HEADS, tile), lambda t, e: (0, t)),
                  pl.BlockSpec((PEER_EBLK, d), lambda t, e: (e, 0)),
                  pl.BlockSpec((d, PEER_EBLK), lambda t, e: (0, e)),
                  pl.BlockSpec((1, d), lambda t, e: (0, 0))],
        out_specs=pl.BlockSpec((tile, d), lambda t, e: (t, 0)),
        out_shape=jax.ShapeDtypeStruct((n_tok, d), F32),
        scratch_shapes=[pltpu.VMEM((d, tile), F32), pltpu.VMEM((PEER_EBLK, tile), BF16)],
        compiler_params=_cparams("parallel", "arbitrary"), name="peer_dense")(
            x, ht, s_t, w_t, tau, u, vt, g_final.reshape(1, d))


def _pad_rows(a, rows):
    return jnp.pad(a, ((0, 0), (rows - a.shape[1], 0), (0, 0)))


def _layer(x3, mem_k, mem_v, a_prev, b_prev, wts, tile):
    bsz, t, d = x3.shape
    n_tok = bsz * t
    x = x3.reshape(n_tok, d)
    za, zb, zif, zg = _norm_proj(x, wts["g_mix"], [wts["w_a"], wts["w_b"], wts["w_if"], wts["w_g"]], tile)

    if a_prev is None:
        band = (BAND_CHUNKS + 1) * CHUNK
        q_pos = np.arange(CHUNK) + BAND_CHUNKS * CHUNK
        bias = _pair_bias(wts["rel_bias"], q_pos, np.arange(band))
        out_a = _band_prompt(za, bias, bsz, t)
        L = CHUNK
        conv_prev = jnp.zeros((bsz, SUBLANES, 2 * B_WIDTH), F32)
        c0 = jnp.zeros((bsz, B_HEADS, B_HEAD_DIM, B_HEAD_DIM), F32)
        n0 = jnp.zeros((bsz, SUBLANES, B_HEAD_DIM), F32)
        m0 = jnp.zeros((bsz, SUBLANES, LANES), F32)
    else:
        k_prev, v_prev = a_prev
        past = k_prev.shape[1]
        za3 = za.reshape(bsz, t, 3 * A_WIDTH)
        k_all = jnp.concatenate([k_prev.reshape(bsz, past, A_WIDTH), za3[:, :, A_WIDTH:2 * A_WIDTH]], axis=1)
        v_all = jnp.concatenate([v_prev.reshape(bsz, past, A_WIDTH), za3[:, :, 2 * A_WIDTH:]], axis=1)
        bias = _pair_bias(wts["rel_bias"], past + np.arange(t), np.arange(past + t))
        out_a = _band_cached(za, k_all, v_all, bias, bsz, t)
        L = t
        conv_prev, c0, n0, m0 = b_prev
        conv_prev = _pad_rows(conv_prev, SUBLANES)
        n0 = jnp.pad(n0, ((0, 0), (0, SUBLANES - B_HEADS), (0, 0)))
        m0 = jnp.broadcast_to(jnp.pad(m0, ((0, 0), (0, SUBLANES - B_HEADS)))[:, :, None], (bsz, SUBLANES, LANES))

    hb, c_new, n_new, m_new = _mlstm(zb, zif, wts["conv_w"], wts["conv_b"], wts["b_if"], wts["g_head"],
                                     conv_prev, c0, n0, m0, bsz, t, L)

    x1 = _merge(x, out_a, hb, zg, wts["w_a_up"], wts["w_b_up"], wts["w_out"], tile)
    x2 = _cross(x1, wts["g_cross"], mem_k, mem_v, wts["w_cq"], wts["w_co"], min(tile, t), t // min(tile, t))

    ht, s_t, w_t, tau = _peer_route(x2, wts["g_ffn"], wts["w_pq_t"], wts["sub_keys"], PEER_TOK)
    y = _peer_dense(x2, ht, s_t, w_t, tau, wts["peer_u"], wts["peer_v_t"], wts["g_final"], PEER_TOK)

    za3 = za.reshape(bsz, t, 3 * A_WIDTH)
    keep = min(BAND_CHUNKS * CHUNK, t) if a_prev is None else t
    new_ak = za3[:, t - keep:, A_WIDTH:2 * A_WIDTH].reshape(bsz, keep, A_HEADS, A_HEAD_DIM)
    new_av = za3[:, t - keep:, 2 * A_WIDTH:].reshape(bsz, keep, A_HEADS, A_HEAD_DIM)
    new_conv = zb.reshape(bsz, t, -1)[:, t - (CONV_W - 1):, :2 * B_WIDTH]
    if t < CONV_W - 1:
        raise NotImplementedError("fewer new frames than the conv history")
    return (y.reshape(bsz, t, d),
            (new_ak, new_av, new_conv, c_new, n_new[:, :B_HEADS, :], m_new[:, :B_HEADS, 0]))


def kernel(x_prompt, x_sample, mem_prompt, cache_a_k, cache_a_v, state_b_conv, state_b_C, state_b_n, state_b_m, cache_mem_k, cache_mem_v, g_mix, w_in, conv_w, conv_b, b_if, g_head, rel_bias, w_a_up, w_b_up, w_out, g_mem, w_mk, w_mv, g_cross, w_cq, w_co, g_ffn, w_pq, sub_keys, peer_u, peer_v, g_final):
    depth = w_in.shape[0]
    assert depth == 1, "the final norm is fused into the last PEER call; one layer supported"
    l = 0
    bsz_p, t_p, d = x_prompt.shape
    bsz_s, t_s, _ = x_sample.shape

    wi = w_in[l]
    o_b = 3 * A_WIDTH
    o_if = o_b + 4 * B_WIDTH
    o_g = o_if + 2 * B_HEADS
    wts = {
        "g_mix": g_mix[l],
        "w_a": wi[:, :o_b].astype(BF16),
        "w_b": wi[:, o_b:o_if].astype(BF16),
        "w_if": jnp.pad(wi[:, o_if:o_g], ((0, 0), (0, IF_PAD - 2 * B_HEADS))).astype(BF16),
        "w_g": wi[:, o_g:].astype(BF16),
        "conv_w": conv_w[l], "conv_b": conv_b[l].reshape(1, -1),
        "b_if": jnp.pad(b_if[l], (0, IF_PAD - 2 * B_HEADS)).reshape(1, IF_PAD),
        "g_head": g_head[l].reshape(1, -1),
        "rel_bias": rel_bias[l],
        "w_a_up": w_a_up[l].astype(BF16), "w_b_up": w_b_up[l].astype(BF16), "w_out": w_out[l].astype(BF16),
        "g_cross": g_cross[l], "w_cq": w_cq[l].astype(BF16), "w_co": w_co[l].astype(BF16),
        "g_ffn": g_ffn[l],
        "w_pq_t": w_pq[l].T.astype(BF16),
        "sub_keys": sub_keys[l].reshape(2 * PEER_HEADS, N_KEYS, PEER_HALF).astype(BF16),
        "peer_u": peer_u[l].astype(BF16),
        "peer_v_t": peer_v[l].T.astype(BF16),
        "g_final": g_final,
    }

    mk_flat, mv_flat = _norm_proj(mem_prompt.reshape(bsz_p * MEM_TOKENS, d), g_mem[l],
                                  [w_mk[l].astype(BF16), w_mv[l].astype(BF16)], 256)
    mk_p = mk_flat.reshape(bsz_p, MEM_TOKENS, d)
    mv_p = mv_flat.reshape(bsz_p, MEM_TOKENS, d)

    yp, sp = _layer(x_prompt, mk_p, mv_p, None, None, wts, 256)
    ys, sq = _layer(x_sample, cache_mem_k[l].reshape(bsz_s, MEM_TOKENS, d), cache_mem_v[l].reshape(bsz_s, MEM_TOKENS, d),
                    (cache_a_k[l], cache_a_v[l]), (state_b_conv[l], state_b_C[l], state_b_n[l], state_b_m[l]), wts, 256)

    ps = sp + (mk_p.reshape(bsz_p, MEM_TOKENS, X_HEADS, X_HEAD_DIM), mv_p.reshape(bsz_p, MEM_TOKENS, X_HEADS, X_HEAD_DIM))
    return (yp, ys) + tuple(a[None] for a in ps) + tuple(a[None] for a in sq)
```

```python
import functools

import numpy as np
import jax
import jax.numpy as jnp
from jax import lax
from jax.experimental import pallas as pl
from jax.experimental.pallas import tpu as pltpu

F32 = jnp.float32
BF16 = jnp.bfloat16

D_MODEL = 1024
CHUNK = 64
A_HEADS = 8
A_HEAD_DIM = 64
A_WIDTH = A_HEADS * A_HEAD_DIM
BAND_CHUNKS = 8
MAX_REL = 128
B_HEADS = 4
B_HEAD_DIM = 128
B_WIDTH = B_HEADS * B_HEAD_DIM
CONV_W = 4
MEM_TOKENS = 256
X_HEADS = 4
X_HEAD_DIM = D_MODEL // X_HEADS
PEER_HEADS = 8
N_KEYS = 128
PEER_TOPK = 16
PEER_QDIM = 256
PEER_HALF = PEER_QDIM // 2
EPS = 1e-6

LANES = 128
SUBLANES = 8
VMEM_LIMIT = 56 * 1024 * 1024
NEG_BIG = -1e30

A_PAIRS = A_HEADS // 2
IF_PAD = LANES
PEER_TOK = 512
PEER_EBLK = 512
PEER_IBLK = PEER_EBLK // N_KEYS


def _cparams(*sem):
    return pltpu.CompilerParams(dimension_semantics=sem, vmem_limit_bytes=VMEM_LIMIT)


def _rms(xf, g):
    r = lax.rsqrt(jnp.mean(xf * xf, axis=-1, keepdims=True) + EPS)
    return xf * r * g


def _sigmoid(x):
    return 1.0 / (1.0 + jnp.exp(-x))


def _log_sigmoid(x):
    return jnp.minimum(x, 0.0) - jnp.log1p(jnp.exp(-jnp.abs(x)))


def _norm_proj_kernel(x_ref, g_ref, *refs):
    n = len(refs) // 2
    h = _rms(x_ref[...], g_ref[...]).astype(BF16)
    for w_ref, o_ref in zip(refs[:n], refs[n:]):
        o_ref[...] = jnp.dot(h, w_ref[...], preferred_element_type=F32)


def _norm_proj(x, g, ws, tile):
    n_tok, d = x.shape
    assert n_tok % tile == 0
    in_specs = [pl.BlockSpec((tile, d), lambda i: (i, 0)), pl.BlockSpec((1, d), lambda i: (0, 0))]
    in_specs += [pl.BlockSpec(w.shape, lambda i: (0, 0)) for w in ws]
    out_specs = [pl.BlockSpec((tile, w.shape[1]), lambda i: (i, 0)) for w in ws]
    out_shape = [jax.ShapeDtypeStruct((n_tok, w.shape[1]), F32) for w in ws]
    return pl.pallas_call(
        _norm_proj_kernel, grid=(n_tok // tile,), in_specs=in_specs, out_specs=out_specs, out_shape=out_shape,
        compiler_params=_cparams("parallel"), name="norm_proj")(x, g.reshape(1, d), *ws)


def _attn_pair(qp, kb, vb, bias, valid_from):
    m_rows = qp.shape[0]
    lane = lax.broadcasted_iota(jnp.int32, qp.shape, 1)
    q0 = jnp.where(lane < A_HEAD_DIM, qp, 0.0)
    q1 = jnp.where(lane >= A_HEAD_DIM, qp, 0.0)
    qs = jnp.concatenate([q0, q1], axis=0).astype(BF16)
    s = lax.dot_general(qs, kb, (((1,), (1,)), ((), ())), preferred_element_type=F32) + bias
    if valid_from is not None:
        col = lax.broadcasted_iota(jnp.int32, s.shape, 1)
        s = jnp.where(col >= valid_from, s, NEG_BIG)
    mx = jnp.max(s, axis=-1, keepdims=True)
    p = jnp.exp(s - mx)
    den = jnp.sum(p, axis=-1, keepdims=True)
    o = jnp.dot(p.astype(BF16), vb, preferred_element_type=F32) / den
    return jnp.where(lane < A_HEAD_DIM, o[:m_rows], o[m_rows:])


def _band_prompt_kernel(q_ref, k_ref, v_ref, bias_ref, o_ref, kpad, vpad):
    c = pl.program_id(1)
    pad = BAND_CHUNKS * CHUNK
    width = pad + CHUNK

    @pl.when(c == 0)
    def _():
        kpad[0:pad, :] = jnp.zeros((pad, A_WIDTH), BF16)
        vpad[0:pad, :] = jnp.zeros((pad, A_WIDTH), BF16)
        kpad[pad:, :] = k_ref[...].astype(BF16)
        vpad[pad:, :] = v_ref[...].astype(BF16)

    start = pl.multiple_of(c * CHUNK, CHUNK)
    valid_from = pad - c * CHUNK
    for hp in range(A_PAIRS):
        cols = slice(hp * LANES, (hp + 1) * LANES)
        qp = q_ref[:, cols] * (A_HEAD_DIM ** -0.5)
        kb = kpad[pl.ds(start, width), cols]
        vb = vpad[pl.ds(start, width), cols]
        o_ref[:, cols] = _attn_pair(qp, kb, vb, bias_ref[hp], valid_from)


def _band_prompt(za, bias, bsz, t):
    nc = t // CHUNK
    pad = BAND_CHUNKS * CHUNK
    return pl.pallas_call(
        _band_prompt_kernel, grid=(bsz, nc),
        in_specs=[pl.BlockSpec((CHUNK, A_WIDTH), lambda b, c: (b * nc + c, 0)),
                  pl.BlockSpec((t, A_WIDTH), lambda b, c: (b, 1)),
                  pl.BlockSpec((t, A_WIDTH), lambda b, c: (b, 2)),
                  pl.BlockSpec(bias.shape, lambda b, c: (0, 0, 0))],
        out_specs=pl.BlockSpec((CHUNK, A_WIDTH), lambda b, c: (b * nc + c, 0)),
        out_shape=jax.ShapeDtypeStruct((bsz * t, A_WIDTH), F32),
        scratch_shapes=[pltpu.VMEM((pad + t, A_WIDTH), BF16), pltpu.VMEM((pad + t, A_WIDTH), BF16)],
        compiler_params=_cparams("parallel", "arbitrary"), name="band_prompt")(za, za, za, bias)


def _band_cached_kernel(q_ref, k_ref, v_ref, bias_ref, o_ref):
    for hp in range(A_PAIRS):
        cols = slice(hp * LANES, (hp + 1) * LANES)
        qp = q_ref[:, cols] * (A_HEAD_DIM ** -0.5)
        kb = k_ref[:, cols].astype(BF16)
        vb = v_ref[:, cols].astype(BF16)
        o_ref[:, cols] = _attn_pair(qp, kb, vb, bias_ref[hp], None)


def _band_cached(za, k_all, v_all, bias, bsz, s_len):
    lk = k_all.shape[1]
    return pl.pallas_call(
        _band_cached_kernel, grid=(bsz,),
        in_specs=[pl.BlockSpec((s_len, A_WIDTH), lambda b: (b, 0)),
                  pl.BlockSpec((None, lk, A_WIDTH), lambda b: (b, 0, 0)),
                  pl.BlockSpec((None, lk, A_WIDTH), lambda b: (b, 0, 0)),
                  pl.BlockSpec(bias.shape, lambda b: (0, 0, 0))],
        out_specs=pl.BlockSpec((s_len, A_WIDTH), lambda b: (b, 0)),
        out_shape=jax.ShapeDtypeStruct((bsz * s_len, A_WIDTH), F32),
        compiler_params=_cparams("parallel"), name="band_cached")(za, k_all, v_all, bias)


def _pair_bias(rel_bias, q_pos, k_pos):
    nq, nk = q_pos.shape[0], k_pos.shape[0]
    assert np.all(np.diff(q_pos) == 1) and np.all(np.diff(k_pos) == 1), "Toeplitz layout needs consecutive positions"
    d_max = int(q_pos[-1] - k_pos[0])
    idx = np.clip(d_max - np.arange(nq + nk - 1), -MAX_REL, MAX_REL) + MAX_REL
    u = jnp.take(rel_bias, jnp.asarray(idx, jnp.int32), axis=1)
    full = jnp.stack([lax.slice_in_dim(u, nq - 1 - q, nq - 1 - q + nk, axis=1) for q in range(nq)], axis=1)
    return full.reshape(A_PAIRS, 2 * nq, nk).astype(F32)


def _mlstm_kernel(zb_ref, zif_ref, convw_ref, convb_ref, bif_ref, ghead_ref, cprev_ref, c0_ref, n0_ref, m0_ref,
                  hb_ref, c_out, n_out, m_out, c_s, n_s, m_s, ubuf):
    c = pl.program_id(1)
    nchunks = pl.num_programs(1)
    L = zif_ref.shape[0]
    qk_w = 2 * B_WIDTH
    hist = SUBLANES

    @pl.when(c == 0)
    def _():
        c_s[...] = c0_ref[...]
        n_s[...] = n0_ref[...]
        m_s[...] = m0_ref[...]
        ubuf[0:hist, :] = cprev_ref[...]

    ubuf[hist:hist + L, :] = zb_ref[:, 0:qk_w]
    acc = jnp.broadcast_to(convb_ref[...], (L, qk_w))
    for j in range(CONV_W):
        off = hist - (CONV_W - 1) + j
        acc = acc + ubuf[off:off + L, :] * convw_ref[j:j + 1, :]
    ubuf[0:hist, :] = ubuf[L:L + hist, :]
    qk = acc * _sigmoid(acc)

    gates = zif_ref[...] + bif_ref[...]
    eye = (lax.broadcasted_iota(jnp.int32, (LANES, LANES), 0) ==
           lax.broadcasted_iota(jnp.int32, (LANES, LANES), 1)).astype(F32)
    gates_t = lax.dot_general(eye, gates, (((1,), (1,)), ((), ())), precision=lax.Precision.HIGHEST,
                              preferred_element_type=F32)
    row = lax.broadcasted_iota(jnp.int32, (L, L), 0)
    colm = lax.broadcasted_iota(jnp.int32, (L, L), 1)
    causal = colm <= row
    lower = causal.astype(F32)
    upper = (row <= colm).astype(F32)
    b_col_all = jnp.dot(lower, _log_sigmoid(gates), precision=lax.Precision.HIGHEST, preferred_element_type=F32)
    b_row_all = jnp.dot(_log_sigmoid(gates_t), upper, precision=lax.Precision.HIGHEST, preferred_element_type=F32)

    for h in range(B_HEADS):
        cols = slice(h * B_HEAD_DIM, (h + 1) * B_HEAD_DIM)
        qh = qk[:, h * B_HEAD_DIM:(h + 1) * B_HEAD_DIM]
        kh = qk[:, B_WIDTH + h * B_HEAD_DIM:B_WIDTH + (h + 1) * B_HEAD_DIM] * (B_HEAD_DIM ** -0.5)
        vh = zb_ref[:, qk_w + h * B_HEAD_DIM:qk_w + (h + 1) * B_HEAD_DIM]
        oh = zb_ref[:, qk_w + B_WIDTH + h * B_HEAD_DIM:qk_w + B_WIDTH + (h + 1) * B_HEAD_DIM]
        qb, kb, vb = qh.astype(BF16), kh.astype(BF16), vh.astype(BF16)

        ig_col = gates[:, h:h + 1]
        ig_row = gates_t[h:h + 1, :]
        b_col = b_col_all[:, B_HEADS + h:B_HEADS + h + 1]
        b_row = b_row_all[B_HEADS + h:B_HEADS + h + 1, :]
        m0 = m_s[h:h + 1, 0:1]
        c0 = c_s[h]
        n0 = n_s[h:h + 1, :]

        dm = jnp.where(causal, b_col - b_row + ig_row, NEG_BIG)
        inter = b_col + m0
        m = jnp.maximum(jnp.max(dm, axis=-1, keepdims=True), inter)
        w_intra = jnp.exp(dm - m)
        w_inter = jnp.exp(inter - m)
        a = lax.dot_general(qb, kb, (((1,), (1,)), ((), ())), preferred_element_type=F32) * w_intra
        num = (jnp.dot(a.astype(BF16), vb, preferred_element_type=F32) +
               w_inter * jnp.dot(qb, c0.astype(BF16), preferred_element_type=F32))
        den = jnp.sum(a, axis=-1, keepdims=True) + w_inter * jnp.sum(qh * n0, axis=-1, keepdims=True)
        hh = num / jnp.maximum(jnp.abs(den), jnp.exp(-m))

        bl = b_col[L - 1:L, :]
        wk = bl - b_col + ig_col
        ml = jnp.maximum(bl + m0, jnp.max(wk, axis=0, keepdims=True))
        a0 = jnp.exp(bl + m0 - ml)
        ws = jnp.exp(wk - ml)
        kw = kh * ws
        c_s[h] = a0 * c0 + lax.dot_general(kw.astype(BF16), vb, (((0,), (0,)), ((), ())), preferred_element_type=F32)
        n_s[h:h + 1, :] = a0 * n0 + jnp.sum(kw, axis=0, keepdims=True)
        m_s[h:h + 1, :] = jnp.broadcast_to(ml, (1, LANES))

        hn = _rms(hh, ghead_ref[:, cols])
        hb_ref[:, cols] = _sigmoid(oh) * hn

    @pl.when(c == nchunks - 1)
    def _():
        c_out[...] = c_s[...]
        n_out[...] = n_s[...]
        m_out[...] = m_s[...]


def _mlstm(zb, zif, conv_w, conv_b, bif, g_head, conv_prev, c0, n0, m0, bsz, t, L):
    nc = t // L
    qk_w = 2 * B_WIDTH
    hd = B_HEAD_DIM
    full2 = lambda b, c: (0, 0)
    outs = pl.pallas_call(
        _mlstm_kernel, grid=(bsz, nc),
        in_specs=[pl.BlockSpec((L, zb.shape[1]), lambda b, c: (b * nc + c, 0)),
                  pl.BlockSpec((L, IF_PAD), lambda b, c: (b * nc + c, 0)),
                  pl.BlockSpec((CONV_W, qk_w), full2),
                  pl.BlockSpec((1, qk_w), full2),
                  pl.BlockSpec((1, IF_PAD), full2),
                  pl.BlockSpec((1, B_WIDTH), full2),
                  pl.BlockSpec((None, SUBLANES, qk_w), lambda b, c: (b, 0, 0)),
                  pl.BlockSpec((None, B_HEADS, hd, hd), lambda b, c: (b, 0, 0, 0)),
                  pl.BlockSpec((None, SUBLANES, hd), lambda b, c: (b, 0, 0)),
                  pl.BlockSpec((None, SUBLANES, LANES), lambda b, c: (b, 0, 0))],
        out_specs=[pl.BlockSpec((L, B_WIDTH), lambda b, c: (b * nc + c, 0)),
                   pl.BlockSpec((None, B_HEADS, hd, hd), lambda b, c: (b, 0, 0, 0)),
                   pl.BlockSpec((None, SUBLANES, hd), lambda b, c: (b, 0, 0)),
                   pl.BlockSpec((None, SUBLANES, LANES), lambda b, c: (b, 0, 0))],
        out_shape=[jax.ShapeDtypeStruct((bsz * t, B_WIDTH), F32),
                   jax.ShapeDtypeStruct((bsz, B_HEADS, hd, hd), F32),
                   jax.ShapeDtypeStruct((bsz, SUBLANES, hd), F32),
                   jax.ShapeDtypeStruct((bsz, SUBLANES, LANES), F32)],
        scratch_shapes=[pltpu.VMEM((B_HEADS, hd, hd), F32), pltpu.VMEM((SUBLANES, hd), F32),
                        pltpu.VMEM((SUBLANES, LANES), F32), pltpu.VMEM((SUBLANES + L, qk_w), F32)],
        compiler_params=_cparams("parallel", "arbitrary"), name="mlstm")(
            zb, zif, conv_w, conv_b, bif, g_head, conv_prev, c0, n0, m0)
    return outs


def _merge_kernel(x_ref, oa_ref, hb_ref, zg_ref, wa_ref, wb_ref, wo_ref, o_ref):
    ua = jnp.dot(oa_ref[...].astype(BF16), wa_ref[...], preferred_element_type=F32)
    ub = jnp.dot(hb_ref[...].astype(BF16), wb_ref[...], preferred_element_type=F32)
    mixed = _sigmoid(zg_ref[:, 0:D_MODEL]) * ua + _sigmoid(zg_ref[:, D_MODEL:]) * ub
    o_ref[...] = x_ref[...] + jnp.dot(mixed.astype(BF16), wo_ref[...], preferred_element_type=F32)


def _merge(x, oa, hb, zg, wa, wb, wo, tile):
    n_tok, d = x.shape
    row = lambda i: (i, 0)
    full = lambda i: (0, 0)
    return pl.pallas_call(
        _merge_kernel, grid=(n_tok // tile,),
        in_specs=[pl.BlockSpec((tile, d), row), pl.BlockSpec((tile, A_WIDTH), row), pl.BlockSpec((tile, B_WIDTH), row),
                  pl.BlockSpec((tile, 2 * d), row), pl.BlockSpec(wa.shape, full), pl.BlockSpec(wb.shape, full),
                  pl.BlockSpec(wo.shape, full)],
        out_specs=pl.BlockSpec((tile, d), row), out_shape=jax.ShapeDtypeStruct((n_tok, d), F32),
        compiler_params=_cparams("parallel"), name="merge")(x, oa, hb, zg, wa, wb, wo)


def _cross_kernel(x_ref, g_ref, k_ref, v_ref, wq_ref, wo_ref, o_ref):
    x = x_ref[...]
    hn = _rms(x, g_ref[...]).astype(BF16)
    q = jnp.dot(hn, wq_ref[...], preferred_element_type=F32) * (X_HEAD_DIM ** -0.5)
    outs = []
    for h in range(X_HEADS):
        cols = slice(h * X_HEAD_DIM, (h + 1) * X_HEAD_DIM)
        qh = q[:, cols].astype(BF16)
        kh = k_ref[:, cols].astype(BF16)
        vh = v_ref[:, cols].astype(BF16)
        s = lax.dot_general(qh, kh, (((1,), (1,)), ((), ())), preferred_element_type=F32)
        mx = jnp.max(s, axis=-1, keepdims=True)
        p = jnp.exp(s - mx)
        den = jnp.sum(p, axis=-1, keepdims=True)
        outs.append((jnp.dot(p.astype(BF16), vh, preferred_element_type=F32) / den).astype(BF16))
    o = jnp.concatenate(outs, axis=-1)
    o_ref[...] = x + jnp.dot(o, wo_ref[...], preferred_element_type=F32)


def _cross(x, g, mem_k, mem_v, wq, wo, tile, tiles_per_batch):
    n_tok, d = x.shape
    row = lambda i: (i, 0)
    full = lambda i: (0, 0)
    mem = lambda i: (i // tiles_per_batch, 0, 0)
    return pl.pallas_call(
        _cross_kernel, grid=(n_tok // tile,),
        in_specs=[pl.BlockSpec((tile, d), row), pl.BlockSpec((1, d), full),
                  pl.BlockSpec((None, MEM_TOKENS, d), mem), pl.BlockSpec((None, MEM_TOKENS, d), mem),
                  pl.BlockSpec(wq.shape, full), pl.BlockSpec(wo.shape, full)],
        out_specs=pl.BlockSpec((tile, d), row), out_shape=jax.ShapeDtypeStruct((n_tok, d), F32),
        compiler_params=_cparams("parallel"), name="cross")(x, g.reshape(1, d), mem_k, mem_v, wq, wo)


def _oddeven_mergesort_pairs(n):
    pairs = []
    p = 1
    while p < n:
        k = p
        while k >= 1:
            for j in range(k % p, n - k, 2 * k):
                for i in range(min(k, n - j - k)):
                    if (i + j) // (2 * p) == (i + j + k) // (2 * p):
                        pairs.append((i + j, i + j + k))
            k //= 2
        p *= 2
    return pairs


_SORT16 = _oddeven_mergesort_pairs(PEER_TOPK)
PEER_ROWCHUNK = 32


def _peer_route_kernel(x_ref, g_ref, wpq_ref, sk_ref, ht_ref, r1_ref, w1_ref, cnt_ref, w0_ref, qt_s, s_s, sv_s, th_s):
    hn = _rms(x_ref[...], g_ref[...])
    ht = hn.T.astype(BF16)
    ht_ref[...] = ht
    qt_s[...] = jnp.dot(wpq_ref[...], ht, preferred_element_type=F32)
    ngrp = N_KEYS // SUBLANES
    assert ngrp == PEER_TOPK

    def score_one(hc):
        off = pl.multiple_of(hc * PEER_HALF, PEER_HALF)
        qt = qt_s[pl.ds(off, PEER_HALF), :].astype(BF16)
        st = jnp.dot(sk_ref[hc], qt, preferred_element_type=F32)
        s_s[hc] = st
        w = [st[SUBLANES * k:SUBLANES * (k + 1), :] for k in range(ngrp)]
        for a, b in _SORT16:
            w[a], w[b] = jnp.maximum(w[a], w[b]), jnp.minimum(w[a], w[b])
        tops = []
        for r in range(PEER_TOPK):
            mx = jnp.max(w[0], axis=0, keepdims=True)
            tops.append(mx)
            hit = w[0] == mx
            for k in range(PEER_TOPK - 1 - r):
                w[k] = jnp.where(hit, w[k + 1], w[k])
        sv_s[hc] = jnp.concatenate(tops, axis=0)

    def score_body(h, carry):
        score_one(2 * h)
        score_one(2 * h + 1)
        return carry

    lax.fori_loop(0, PEER_HEADS, score_body, 0)

    def head_body(h, carry):
        sv0 = sv_s[2 * h]
        sv1 = sv_s[2 * h + 1]
        half = PEER_TOPK // 2
        pieces = [sv0[0:1] + sv1, sv0[1:2] + sv1[0:half]]
        pieces += [sv0[a:a + 1] + sv1[0:half] for a in range(2, half)]
        pieces += [sv0[half:] + sv1[0:1]]
        cand = jnp.concatenate(pieces, axis=0)
        top = sv0[0:1] + sv1[0:1]
        z = jnp.zeros_like(top)
        tau = top
        for _ in range(PEER_TOPK):
            tau = jnp.max(cand, axis=0, keepdims=True)
            z = z + jnp.exp(tau - top)
            cand = jnp.where(cand == tau, NEG_BIG, cand)
        for b in range(PEER_TOPK):
            sel = (sv0 + sv1[b:b + 1]) >= tau
            th_s[b:b + 1, :] = jnp.min(jnp.where(sel, sv0, -NEG_BIG), axis=0, keepdims=True)
        inv_z = 1.0 / z
        for kk in range(0, N_KEYS, PEER_ROWCHUNK):
            rows = slice(kk, kk + PEER_ROWCHUNK)
            s0 = s_s[2 * h, rows, :]
            s1 = s_s[2 * h + 1, rows, :]
            cnt = jnp.zeros_like(s0)
            rank = jnp.zeros_like(s1)
            for b in range(PEER_TOPK):
                cnt = jnp.where(s0 >= th_s[b:b + 1, :], float(b + 1), cnt)
                rank = jnp.where(sv1[b:b + 1] > s1, float(b + 1), rank)
            cnt_ref[h, rows, :] = cnt
            r1_ref[h, rows, :] = rank.astype(BF16)
            w0_ref[h, rows, :] = jnp.exp(s0 - sv0[0:1])
            w1_ref[h, rows, :] = (jnp.exp(s1 - sv1[0:1]) * inv_z).astype(BF16)
        return carry

    lax.fori_loop(0, PEER_HEADS, head_body, 0)


def _peer_route(x, g, wpq_t, sk, tile):
    n_tok, d = x.shape
    nhc = 2 * PEER_HEADS
    col3 = lambda i: (0, 0, i)
    hshape = (PEER_HEADS, N_KEYS, n_tok)
    hblock = pl.BlockSpec((PEER_HEADS, N_KEYS, tile), col3)
    return pl.pallas_call(
        _peer_route_kernel, grid=(n_tok // tile,),
        in_specs=[pl.BlockSpec((tile, d), lambda i: (i, 0)), pl.BlockSpec((1, d), lambda i: (0, 0)),
                  pl.BlockSpec(wpq_t.shape, lambda i: (0, 0)), pl.BlockSpec(sk.shape, lambda i: (0, 0, 0))],
        out_specs=[pl.BlockSpec((d, tile), lambda i: (0, i)), hblock, hblock, hblock, hblock],
        out_shape=[jax.ShapeDtypeStruct((d, n_tok), BF16),
                   jax.ShapeDtypeStruct(hshape, BF16), jax.ShapeDtypeStruct(hshape, BF16),
                   jax.ShapeDtypeStruct(hshape, F32), jax.ShapeDtypeStruct(hshape, F32)],
        scratch_shapes=[pltpu.VMEM((PEER_HEADS * PEER_QDIM, tile), F32), pltpu.VMEM((nhc, N_KEYS, tile), F32),
                        pltpu.VMEM((nhc, PEER_TOPK, tile), F32), pltpu.VMEM((PEER_TOPK, tile), F32)],
        compiler_params=_cparams("parallel"), name="peer_route")(x, g.reshape(1, d), wpq_t, sk)


def _gelu(z):
    return 0.5 * z * (1.0 + lax.erf(z * (2.0 ** -0.5)))


BF16_ROWS = 2 * SUBLANES


def _peer_dense_kernel(x_ref, ht_ref, r1_ref, w1_ref, cnt_ref, w0_ref, u_ref, vt_ref, gf_ref, y_ref,
                       acc_s, act_s, c_s):
    s = pl.program_id(1)
    nblk = pl.num_programs(1) - 2
    P = x_ref.shape[0]

    @pl.when(s == 0)
    def _():
        acc_s[...] = jnp.zeros_like(acc_s)
        act_s[...] = jnp.zeros_like(act_s)
        c_s[...] = jnp.zeros_like(c_s)

    blk = jnp.clip(s - 1, 0, nblk - 1)
    slab_l = 2 * LANES

    def gate_rows(ii, act_cur, c_cur):
        i = blk * PEER_IBLK + ii
        cnt_rows = [cnt_ref[h, pl.ds(i, 1), :] for h in range(PEER_HEADS)]
        w0_rows = [w0_ref[h, pl.ds(i, 1), :] for h in range(PEER_HEADS)]
        for lg in range(P // slab_l):
            ls = slice(lg * slab_l, (lg + 1) * slab_l)
            cnt_b = [jnp.broadcast_to(r[:, ls], (SUBLANES, slab_l)).astype(BF16) for r in cnt_rows]
            w0_b = [jnp.broadcast_to(r[:, ls], (SUBLANES, slab_l)).astype(BF16) for r in w0_rows]
            for jg in range(N_KEYS // BF16_ROWS):
                halves = []
                for hf in range(2):
                    j0 = jg * BF16_ROWS + hf * SUBLANES
                    js = slice(j0, j0 + SUBLANES)
                    g = None
                    for h in range(PEER_HEADS):
                        w1 = w1_ref[h, js, ls]
                        t = jnp.where(r1_ref[h, js, ls] < cnt_b[h], w1, jnp.zeros_like(w1)) * w0_b[h]
                        g = t if g is None else g + t
                    halves.append(g)
                r0 = ii * N_KEYS + jg * BF16_ROWS
                rows = slice(r0, r0 + BF16_ROWS)
                c_cur[rows, ls] = jnp.concatenate(halves, axis=0) * act_cur[rows, ls]

    cur = lax.rem(s + 1, 2)
    prev = lax.rem(s, 2)
    acc_s[...] += jnp.dot(vt_ref[...], c_s[prev], preferred_element_type=F32)
    for ii in range(PEER_IBLK):
        gate_rows(ii, act_s.at[cur], c_s.at[cur])
    act_s[prev] = _gelu(jnp.dot(u_ref[...], ht_ref[...], preferred_element_type=F32)).astype(BF16)

    @pl.when(s == nblk + 1)
    def _():
        y = x_ref[...] + acc_s[...].T
        y_ref[...] = _rms(y, gf_ref[...])


def _peer_dense(x, ht, r1, w1, cnt, w0, u, vt, g_final, tile):
    n_tok, d = x.shape
    n_exp = u.shape[0]
    nblk = n_exp // PEER_EBLK
    hblock = pl.BlockSpec((PEER_HEADS, N_KEYS, tile), lambda t, s: (0, 0, t))
    return pl.pallas_call(
        _peer_dense_kernel, grid=(n_tok // tile, nblk + 2),
        in_specs=[pl.BlockSpec((tile, d), lambda t, s: (t, 0)),
                  pl.BlockSpec((d, tile), lambda t, s: (0, t)),
                  hblock, hblock, hblock, hblock,
                  pl.BlockSpec((PEER_EBLK, d), lambda t, s: (jnp.minimum(s, nblk - 1), 0)),
                  pl.BlockSpec((d, PEER_EBLK), lambda t, s: (0, jnp.clip(s - 2, 0, nblk - 1))),
                  pl.BlockSpec((1, d), lambda t, s: (0, 0))],
        out_specs=pl.BlockSpec((tile, d), lambda t, s: (t, 0)),
        out_shape=jax.ShapeDtypeStruct((n_tok, d), F32),
        scratch_shapes=[pltpu.VMEM((d, tile), F32), pltpu.VMEM((2, PEER_EBLK, tile), BF16),
                        pltpu.VMEM((2, PEER_EBLK, tile), BF16)],
        compiler_params=_cparams("parallel", "arbitrary"), name="peer_dense")(
            x, ht, r1, w1, cnt, w0, u, vt, g_final.reshape(1, d))


def _pad_rows(a, rows):
    return jnp.pad(a, ((0, 0), (rows - a.shape[1], 0), (0, 0)))


def _layer(x3, mem_k, mem_v, a_prev, b_prev, wts, tile):
    bsz, t, d = x3.shape
    n_tok = bsz * t
    x = x3.reshape(n_tok, d)
    za, zb, zif, zg = _norm_proj(x, wts["g_mix"], [wts["w_a"], wts["w_b"], wts["w_if"], wts["w_g"]], tile)

    if a_prev is None:
        band = (BAND_CHUNKS + 1) * CHUNK
        q_pos = np.arange(CHUNK) + BAND_CHUNKS * CHUNK
        bias = _pair_bias(wts["rel_bias"], q_pos, np.arange(band))
        out_a = _band_prompt(za, bias, bsz, t)
        L = CHUNK
        conv_prev = jnp.zeros((bsz, SUBLANES, 2 * B_WIDTH), F32)
        c0 = jnp.zeros((bsz, B_HEADS, B_HEAD_DIM, B_HEAD_DIM), F32)
        n0 = jnp.zeros((bsz, SUBLANES, B_HEAD_DIM), F32)
        m0 = jnp.zeros((bsz, SUBLANES, LANES), F32)
    else:
        k_prev, v_prev = a_prev
        past = k_prev.shape[1]
        za3 = za.reshape(bsz, t, 3 * A_WIDTH)
        k_all = jnp.concatenate([k_prev.reshape(bsz, past, A_WIDTH), za3[:, :, A_WIDTH:2 * A_WIDTH]], axis=1)
        v_all = jnp.concatenate([v_prev.reshape(bsz, past, A_WIDTH), za3[:, :, 2 * A_WIDTH:]], axis=1)
        bias = _pair_bias(wts["rel_bias"], past + np.arange(t), np.arange(past + t))
        out_a = _band_cached(za, k_all, v_all, bias, bsz, t)
        L = t
        conv_prev, c0, n0, m0 = b_prev
        conv_prev = _pad_rows(conv_prev, SUBLANES)
        n0 = jnp.pad(n0, ((0, 0), (0, SUBLANES - B_HEADS), (0, 0)))
        m0 = jnp.broadcast_to(jnp.pad(m0, ((0, 0), (0, SUBLANES - B_HEADS)))[:, :, None], (bsz, SUBLANES, LANES))

    hb, c_new, n_new, m_new = _mlstm(zb, zif, wts["conv_w"], wts["conv_b"], wts["b_if"], wts["g_head"],
                                     conv_prev, c0, n0, m0, bsz, t, L)

    x1 = _merge(x, out_a, hb, zg, wts["w_a_up"], wts["w_b_up"], wts["w_out"], tile)
    x2 = _cross(x1, wts["g_cross"], mem_k, mem_v, wts["w_cq"], wts["w_co"], min(tile, t), t // min(tile, t))

    ht, r1, w1, cnt, w0 = _peer_route(x2, wts["g_ffn"], wts["w_pq_t"], wts["sub_keys"], PEER_TOK)
    y = _peer_dense(x2, ht, r1, w1, cnt, w0, wts["peer_u"], wts["peer_v_t"], wts["g_final"], PEER_TOK)

    za3 = za.reshape(bsz, t, 3 * A_WIDTH)
    keep = min(BAND_CHUNKS * CHUNK, t) if a_prev is None else t
    new_ak = za3[:, t - keep:, A_WIDTH:2 * A_WIDTH].reshape(bsz, keep, A_HEADS, A_HEAD_DIM)
    new_av = za3[:, t - keep:, 2 * A_WIDTH:].reshape(bsz, keep, A_HEADS, A_HEAD_DIM)
    new_conv = zb.reshape(bsz, t, -1)[:, t - (CONV_W - 1):, :2 * B_WIDTH]
    if t < CONV_W - 1:
        raise NotImplementedError("fewer new frames than the conv history")
    return (y.reshape(bsz, t, d),
            (new_ak, new_av, new_conv, c_new, n_new[:, :B_HEADS, :], m_new[:, :B_HEADS, 0]))


def kernel(x_prompt, x_sample, mem_prompt, cache_a_k, cache_a_v, state_b_conv, state_b_C, state_b_n, state_b_m, cache_mem_k, cache_mem_v, g_mix, w_in, conv_w, conv_b, b_if, g_head, rel_bias, w_a_up, w_b_up, w_out, g_mem, w_mk, w_mv, g_cross, w_cq, w_co, g_ffn, w_pq, sub_keys, peer_u, peer_v, g_final):
    depth = w_in.shape[0]
    assert depth == 1, "the final norm is fused into the last PEER call; one layer supported"
    l = 0
    bsz_p, t_p, d = x_prompt.shape
    bsz_s, t_s, _ = x_sample.shape

    wi = w_in[l]
    o_b = 3 * A_WIDTH
    o_if = o_b + 4 * B_WIDTH
    o_g = o_if + 2 * B_HEADS
    wts = {
        "g_mix": g_mix[l],
        "w_a": wi[:, :o_b].astype(BF16),
        "w_b": wi[:, o_b:o_if].astype(BF16),
        "w_if": jnp.pad(wi[:, o_if:o_g], ((0, 0), (0, IF_PAD - 2 * B_HEADS))).astype(BF16),
        "w_g": wi[:, o_g:].astype(BF16),
        "conv_w": conv_w[l], "conv_b": conv_b[l].reshape(1, -1),
        "b_if": jnp.pad(b_if[l], (0, IF_PAD - 2 * B_HEADS)).reshape(1, IF_PAD),
        "g_head": g_head[l].reshape(1, -1),
        "rel_bias": rel_bias[l],
        "w_a_up": w_a_up[l].astype(BF16), "w_b_up": w_b_up[l].astype(BF16), "w_out": w_out[l].astype(BF16),
        "g_cross": g_cross[l], "w_cq": w_cq[l].astype(BF16), "w_co": w_co[l].astype(BF16),
        "g_ffn": g_ffn[l],
        "w_pq_t": w_pq[l].T.astype(BF16),
        "sub_keys": sub_keys[l].reshape(2 * PEER_HEADS, N_KEYS, PEER_HALF).astype(BF16),
        "peer_u": peer_u[l].astype(BF16),
        "peer_v_t": peer_v[l].T.astype(BF16),
        "g_final": g_final,
    }

    mk_flat, mv_flat = _norm_proj(mem_prompt.reshape(bsz_p * MEM_TOKENS, d), g_mem[l],
                                  [w_mk[l].astype(BF16), w_mv[l].astype(BF16)], 256)
    mk_p = mk_flat.reshape(bsz_p, MEM_TOKENS, d)
    mv_p = mv_flat.reshape(bsz_p, MEM_TOKENS, d)

    yp, sp = _layer(x_prompt, mk_p, mv_p, None, None, wts, 256)
    ys, sq = _layer(x_sample, cache_mem_k[l].reshape(bsz_s, MEM_TOKENS, d), cache_mem_v[l].reshape(bsz_s, MEM_TOKENS, d),
                    (cache_a_k[l], cache_a_v[l]), (state_b_conv[l], state_b_C[l], state_b_n[l], state_b_m[l]), wts, 256)

    ps = sp + (mk_p.reshape(bsz_p, MEM_TOKENS, X_HEADS, X_HEAD_DIM), mv_p.reshape(bsz_p, MEM_TOKENS, X_HEADS, X_HEAD_DIM))
    return (yp, ys) + tuple(a[None] for a in ps) + tuple(a[None] for a in sq)
```

```python
import functools

import numpy as np
import jax
import jax.numpy as jnp
from jax import lax
from jax.experimental import pallas as pl
from jax.experimental.pallas import tpu as pltpu

F32 = jnp.float32
BF16 = jnp.bfloat16

D_MODEL = 1024
CHUNK = 64
A_HEADS = 8
A_HEAD_DIM = 64
A_WIDTH = A_HEADS * A_HEAD_DIM
BAND_CHUNKS = 8
MAX_REL = 128
B_HEADS = 4
B_HEAD_DIM = 128
B_WIDTH = B_HEADS * B_HEAD_DIM
CONV_W = 4
MEM_TOKENS = 256
X_HEADS = 4
X_HEAD_DIM = D_MODEL // X_HEADS
PEER_HEADS = 8
N_KEYS = 128
PEER_TOPK = 16
PEER_QDIM = 256
PEER_HALF = PEER_QDIM // 2
EPS = 1e-6

LANES = 128
SUBLANES = 8
VMEM_LIMIT = 56 * 1024 * 1024
NEG_BIG = -1e30

A_PAIRS = A_HEADS // 2
IF_PAD = LANES
PEER_TOK = 512
PEER_EBLK = 512
PEER_IBLK = PEER_EBLK // N_KEYS


def _cparams(*sem):
    return pltpu.CompilerParams(dimension_semantics=sem, vmem_limit_bytes=VMEM_LIMIT)


def _rms(xf, g):
    r = lax.rsqrt(jnp.mean(xf * xf, axis=-1, keepdims=True) + EPS)
    return xf * r * g


def _sigmoid(x):
    return 1.0 / (1.0 + jnp.exp(-x))


def _log_sigmoid(x):
    return jnp.minimum(x, 0.0) - jnp.log1p(jnp.exp(-jnp.abs(x)))


def _norm_proj_kernel(x_ref, g_ref, *refs):
    n = len(refs) // 2
    h = _rms(x_ref[...], g_ref[...]).astype(BF16)
    for w_ref, o_ref in zip(refs[:n], refs[n:]):
        o_ref[...] = jnp.dot(h, w_ref[...], preferred_element_type=F32)


def _norm_proj(x, g, ws, tile):
    n_tok, d = x.shape
    assert n_tok % tile == 0
    in_specs = [pl.BlockSpec((tile, d), lambda i: (i, 0)), pl.BlockSpec((1, d), lambda i: (0, 0))]
    in_specs += [pl.BlockSpec(w.shape, lambda i: (0, 0)) for w in ws]
    out_specs = [pl.BlockSpec((tile, w.shape[1]), lambda i: (i, 0)) for w in ws]
    out_shape = [jax.ShapeDtypeStruct((n_tok, w.shape[1]), F32) for w in ws]
    return pl.pallas_call(
        _norm_proj_kernel, grid=(n_tok // tile,), in_specs=in_specs, out_specs=out_specs, out_shape=out_shape,
        compiler_params=_cparams("parallel"), name="norm_proj")(x, g.reshape(1, d), *ws)


def _attn_pair(qp, kb, vb, bias, valid_from):
    m_rows = qp.shape[0]
    lane = lax.broadcasted_iota(jnp.int32, qp.shape, 1)
    q0 = jnp.where(lane < A_HEAD_DIM, qp, 0.0)
    q1 = jnp.where(lane >= A_HEAD_DIM, qp, 0.0)
    qs = jnp.concatenate([q0, q1], axis=0).astype(BF16)
    s = lax.dot_general(qs, kb, (((1,), (1,)), ((), ())), preferred_element_type=F32) + bias
    if valid_from is not None:
        col = lax.broadcasted_iota(jnp.int32, s.shape, 1)
        s = jnp.where(col >= valid_from, s, NEG_BIG)
    mx = jnp.max(s, axis=-1, keepdims=True)
    p = jnp.exp(s - mx)
    den = jnp.sum(p, axis=-1, keepdims=True)
    o = jnp.dot(p.astype(BF16), vb, preferred_element_type=F32) / den
    return jnp.where(lane < A_HEAD_DIM, o[:m_rows], o[m_rows:])


def _band_prompt_kernel(q_ref, k_ref, v_ref, bias_ref, o_ref, kpad, vpad):
    c = pl.program_id(1)
    pad = BAND_CHUNKS * CHUNK
    width = pad + CHUNK

    @pl.when(c == 0)
    def _():
        kpad[0:pad, :] = jnp.zeros((pad, A_WIDTH), BF16)
        vpad[0:pad, :] = jnp.zeros((pad, A_WIDTH), BF16)
        kpad[pad:, :] = k_ref[...].astype(BF16)
        vpad[pad:, :] = v_ref[...].astype(BF16)

    for k in range(q_ref.shape[0] // CHUNK):
        cc = c * (q_ref.shape[0] // CHUNK) + k
        rows = slice(k * CHUNK, (k + 1) * CHUNK)
        start = pl.multiple_of(cc * CHUNK, CHUNK)
        valid_from = pad - cc * CHUNK
        for hp in range(A_PAIRS):
            cols = slice(hp * LANES, (hp + 1) * LANES)
            qp = q_ref[rows, cols] * (A_HEAD_DIM ** -0.5)
            kb = kpad[pl.ds(start, width), cols]
            vb = vpad[pl.ds(start, width), cols]
            o_ref[rows, cols] = _attn_pair(qp, kb, vb, bias_ref[hp], valid_from)


BAND_CHUNKS_PER_STEP = 4


def _band_prompt(za, bias, bsz, t):
    per = BAND_CHUNKS_PER_STEP if (t // CHUNK) % BAND_CHUNKS_PER_STEP == 0 else 1
    nc = t // (CHUNK * per)
    pad = BAND_CHUNKS * CHUNK
    return pl.pallas_call(
        _band_prompt_kernel, grid=(bsz, nc),
        in_specs=[pl.BlockSpec((CHUNK * per, A_WIDTH), lambda b, c: (b * nc + c, 0)),
                  pl.BlockSpec((t, A_WIDTH), lambda b, c: (b, 1)),
                  pl.BlockSpec((t, A_WIDTH), lambda b, c: (b, 2)),
                  pl.BlockSpec(bias.shape, lambda b, c: (0, 0, 0))],
        out_specs=pl.BlockSpec((CHUNK * per, A_WIDTH), lambda b, c: (b * nc + c, 0)),
        out_shape=jax.ShapeDtypeStruct((bsz * t, A_WIDTH), F32),
        scratch_shapes=[pltpu.VMEM((pad + t, A_WIDTH), BF16), pltpu.VMEM((pad + t, A_WIDTH), BF16)],
        compiler_params=_cparams("parallel", "arbitrary"), name="band_prompt")(za, za, za, bias)


def _band_cached_kernel(q_ref, k_ref, v_ref, bias_ref, o_ref):
    for hp in range(A_PAIRS):
        cols = slice(hp * LANES, (hp + 1) * LANES)
        qp = q_ref[:, cols] * (A_HEAD_DIM ** -0.5)
        kb = k_ref[:, cols].astype(BF16)
        vb = v_ref[:, cols].astype(BF16)
        o_ref[:, cols] = _attn_pair(qp, kb, vb, bias_ref[hp], None)


def _band_cached(za, k_all, v_all, bias, bsz, s_len):
    lk = k_all.shape[1]
    return pl.pallas_call(
        _band_cached_kernel, grid=(bsz,),
        in_specs=[pl.BlockSpec((s_len, A_WIDTH), lambda b: (b, 0)),
                  pl.BlockSpec((None, lk, A_WIDTH), lambda b: (b, 0, 0)),
                  pl.BlockSpec((None, lk, A_WIDTH), lambda b: (b, 0, 0)),
                  pl.BlockSpec(bias.shape, lambda b: (0, 0, 0))],
        out_specs=pl.BlockSpec((s_len, A_WIDTH), lambda b: (b, 0)),
        out_shape=jax.ShapeDtypeStruct((bsz * s_len, A_WIDTH), F32),
        compiler_params=_cparams("parallel"), name="band_cached")(za, k_all, v_all, bias)


def _pair_bias(rel_bias, q_pos, k_pos):
    nq, nk = q_pos.shape[0], k_pos.shape[0]
    assert np.all(np.diff(q_pos) == 1) and np.all(np.diff(k_pos) == 1), "Toeplitz layout needs consecutive positions"
    d_max = int(q_pos[-1] - k_pos[0])
    idx = np.clip(d_max - np.arange(nq + nk - 1), -MAX_REL, MAX_REL) + MAX_REL
    u = jnp.take(rel_bias, jnp.asarray(idx, jnp.int32), axis=1)
    full = jnp.stack([lax.slice_in_dim(u, nq - 1 - q, nq - 1 - q + nk, axis=1) for q in range(nq)], axis=1)
    return full.reshape(A_PAIRS, 2 * nq, nk).astype(F32)


def _mlstm_kernel(zb_all, zif_all, convw_ref, convb_ref, bif_ref, ghead_ref, cprev_ref, c0_ref, n0_ref, m0_ref,
                  hb_all, c_out, n_out, m_out, c_all, n_all, m_all, ubuf_all):
    c = pl.program_id(1)
    nchunks = pl.num_programs(1)
    nseq, L = zif_all.shape[0], zif_all.shape[1]
    hist = SUBLANES

    @pl.when(c == 0)
    def _():
        c_all[...] = c0_ref[...]
        n_all[...] = n0_ref[...]
        m_all[...] = m0_ref[...]
        ubuf_all[:, 0:hist, :] = cprev_ref[...]

    for bb in range(nseq):
        _mlstm_chunk(zb_all.at[bb], zif_all.at[bb], convw_ref, convb_ref, bif_ref, ghead_ref, hb_all.at[bb],
                     c_all.at[bb], n_all.at[bb], m_all.at[bb], ubuf_all.at[bb], L)

    @pl.when(c == nchunks - 1)
    def _():
        c_out[...] = c_all[...]
        n_out[...] = n_all[...]
        m_out[...] = m_all[...]


def _mlstm_chunk(zb_ref, zif_ref, convw_ref, convb_ref, bif_ref, ghead_ref, hb_ref, c_s, n_s, m_s, ubuf, L):
    qk_w = 2 * B_WIDTH
    hist = SUBLANES
    ubuf[hist:hist + L, :] = zb_ref[:, 0:qk_w]
    acc = jnp.broadcast_to(convb_ref[...], (L, qk_w))
    for j in range(CONV_W):
        off = hist - (CONV_W - 1) + j
        acc = acc + ubuf[off:off + L, :] * convw_ref[j:j + 1, :]
    ubuf[0:hist, :] = ubuf[L:L + hist, :]
    qk = acc * _sigmoid(acc)

    gates = zif_ref[...] + bif_ref[...]
    eye = (lax.broadcasted_iota(jnp.int32, (LANES, LANES), 0) ==
           lax.broadcasted_iota(jnp.int32, (LANES, LANES), 1)).astype(F32)
    gates_t = lax.dot_general(eye, gates, (((1,), (1,)), ((), ())), precision=lax.Precision.HIGHEST,
                              preferred_element_type=F32)
    row = lax.broadcasted_iota(jnp.int32, (L, L), 0)
    colm = lax.broadcasted_iota(jnp.int32, (L, L), 1)
    causal = colm <= row
    lower = causal.astype(F32)
    upper = (row <= colm).astype(F32)
    b_col_all = jnp.dot(lower, _log_sigmoid(gates), precision=lax.Precision.HIGHEST, preferred_element_type=F32)
    b_row_all = jnp.dot(_log_sigmoid(gates_t), upper, precision=lax.Precision.HIGHEST, preferred_element_type=F32)

    for h in range(B_HEADS):
        cols = slice(h * B_HEAD_DIM, (h + 1) * B_HEAD_DIM)
        qh = qk[:, h * B_HEAD_DIM:(h + 1) * B_HEAD_DIM]
        kh = qk[:, B_WIDTH + h * B_HEAD_DIM:B_WIDTH + (h + 1) * B_HEAD_DIM] * (B_HEAD_DIM ** -0.5)
        vh = zb_ref[:, qk_w + h * B_HEAD_DIM:qk_w + (h + 1) * B_HEAD_DIM]
        oh = zb_ref[:, qk_w + B_WIDTH + h * B_HEAD_DIM:qk_w + B_WIDTH + (h + 1) * B_HEAD_DIM]
        qb, kb, vb = qh.astype(BF16), kh.astype(BF16), vh.astype(BF16)

        ig_col = gates[:, h:h + 1]
        ig_row = gates_t[h:h + 1, :]
        b_col = b_col_all[:, B_HEADS + h:B_HEADS + h + 1]
        b_row = b_row_all[B_HEADS + h:B_HEADS + h + 1, :]
        m0 = m_s[h:h + 1, 0:1]
        c0 = c_s[h]
        n0 = n_s[h:h + 1, :]

        dm = jnp.where(causal, b_col - b_row + ig_row, NEG_BIG)
        inter = b_col + m0
        m = jnp.maximum(jnp.max(dm, axis=-1, keepdims=True), inter)
        w_intra = jnp.exp(dm - m)
        w_inter = jnp.exp(inter - m)
        a = lax.dot_general(qb, kb, (((1,), (1,)), ((), ())), preferred_element_type=F32) * w_intra
        num = (jnp.dot(a.astype(BF16), vb, preferred_element_type=F32) +
               w_inter * jnp.dot(qb, c0.astype(BF16), preferred_element_type=F32))
        den = jnp.sum(a, axis=-1, keepdims=True) + w_inter * jnp.sum(qh * n0, axis=-1, keepdims=True)
        hh = num / jnp.maximum(jnp.abs(den), jnp.exp(-m))

        bl = b_col[L - 1:L, :]
        wk = bl - b_col + ig_col
        ml = jnp.maximum(bl + m0, jnp.max(wk, axis=0, keepdims=True))
        a0 = jnp.exp(bl + m0 - ml)
        ws = jnp.exp(wk - ml)
        kw = kh * ws
        c_s[h] = a0 * c0 + lax.dot_general(kw.astype(BF16), vb, (((0,), (0,)), ((), ())), preferred_element_type=F32)
        n_s[h:h + 1, :] = a0 * n0 + jnp.sum(kw, axis=0, keepdims=True)
        m_s[h:h + 1, :] = jnp.broadcast_to(ml, (1, LANES))

        hn = _rms(hh, ghead_ref[:, cols])
        hb_ref[:, cols] = _sigmoid(oh) * hn


MLSTM_SEQ_PER_STEP = 4


def _mlstm(zb, zif, conv_w, conv_b, bif, g_head, conv_prev, c0, n0, m0, bsz, t, L):
    nc = t // L
    qk_w = 2 * B_WIDTH
    hd = B_HEAD_DIM
    g = next(k for k in (MLSTM_SEQ_PER_STEP, 2, 1) if bsz % k == 0)
    full2 = lambda b, c: (0, 0)
    seq3 = lambda b, c: (b, c, 0)
    st3 = lambda b, c: (b, 0, 0)
    outs = pl.pallas_call(
        _mlstm_kernel, grid=(bsz // g, nc),
        in_specs=[pl.BlockSpec((g, L, zb.shape[1]), seq3),
                  pl.BlockSpec((g, L, IF_PAD), seq3),
                  pl.BlockSpec((CONV_W, qk_w), full2),
                  pl.BlockSpec((1, qk_w), full2),
                  pl.BlockSpec((1, IF_PAD), full2),
                  pl.BlockSpec((1, B_WIDTH), full2),
                  pl.BlockSpec((g, SUBLANES, qk_w), st3),
                  pl.BlockSpec((g, B_HEADS, hd, hd), lambda b, c: (b, 0, 0, 0)),
                  pl.BlockSpec((g, SUBLANES, hd), st3),
                  pl.BlockSpec((g, SUBLANES, LANES), st3)],
        out_specs=[pl.BlockSpec((g, L, B_WIDTH), seq3),
                   pl.BlockSpec((g, B_HEADS, hd, hd), lambda b, c: (b, 0, 0, 0)),
                   pl.BlockSpec((g, SUBLANES, hd), st3),
                   pl.BlockSpec((g, SUBLANES, LANES), st3)],
        out_shape=[jax.ShapeDtypeStruct((bsz, t, B_WIDTH), F32),
                   jax.ShapeDtypeStruct((bsz, B_HEADS, hd, hd), F32),
                   jax.ShapeDtypeStruct((bsz, SUBLANES, hd), F32),
                   jax.ShapeDtypeStruct((bsz, SUBLANES, LANES), F32)],
        scratch_shapes=[pltpu.VMEM((g, B_HEADS, hd, hd), F32), pltpu.VMEM((g, SUBLANES, hd), F32),
                        pltpu.VMEM((g, SUBLANES, LANES), F32), pltpu.VMEM((g, SUBLANES + L, qk_w), F32)],
        compiler_params=_cparams("parallel", "arbitrary"), name="mlstm")(
            zb.reshape(bsz, t, -1), zif.reshape(bsz, t, -1), conv_w, conv_b, bif, g_head, conv_prev, c0, n0, m0)
    return (outs[0].reshape(bsz * t, B_WIDTH),) + tuple(outs[1:])


def _merge_kernel(x_ref, oa_ref, hb_ref, zg_ref, wa_ref, wb_ref, wo_ref, o_ref):
    ua = jnp.dot(oa_ref[...].astype(BF16), wa_ref[...], preferred_element_type=F32)
    ub = jnp.dot(hb_ref[...].astype(BF16), wb_ref[...], preferred_element_type=F32)
    mixed = _sigmoid(zg_ref[:, 0:D_MODEL]) * ua + _sigmoid(zg_ref[:, D_MODEL:]) * ub
    o_ref[...] = x_ref[...] + jnp.dot(mixed.astype(BF16), wo_ref[...], preferred_element_type=F32)


def _merge(x, oa, hb, zg, wa, wb, wo, tile):
    n_tok, d = x.shape
    row = lambda i: (i, 0)
    full = lambda i: (0, 0)
    return pl.pallas_call(
        _merge_kernel, grid=(n_tok // tile,),
        in_specs=[pl.BlockSpec((tile, d), row), pl.BlockSpec((tile, A_WIDTH), row), pl.BlockSpec((tile, B_WIDTH), row),
                  pl.BlockSpec((tile, 2 * d), row), pl.BlockSpec(wa.shape, full), pl.BlockSpec(wb.shape, full),
                  pl.BlockSpec(wo.shape, full)],
        out_specs=pl.BlockSpec((tile, d), row), out_shape=jax.ShapeDtypeStruct((n_tok, d), F32),
        compiler_params=_cparams("parallel"), name="merge")(x, oa, hb, zg, wa, wb, wo)


def _cross_kernel(x_ref, g_ref, k_ref, v_ref, wq_ref, wo_ref, o_ref):
    x = x_ref[...]
    hn = _rms(x, g_ref[...]).astype(BF16)
    q = jnp.dot(hn, wq_ref[...], preferred_element_type=F32) * (X_HEAD_DIM ** -0.5)
    outs = []
    for h in range(X_HEADS):
        cols = slice(h * X_HEAD_DIM, (h + 1) * X_HEAD_DIM)
        qh = q[:, cols].astype(BF16)
        kh = k_ref[:, cols].astype(BF16)
        vh = v_ref[:, cols].astype(BF16)
        s = lax.dot_general(qh, kh, (((1,), (1,)), ((), ())), preferred_element_type=F32)
        mx = jnp.max(s, axis=-1, keepdims=True)
        p = jnp.exp(s - mx)
        den = jnp.sum(p, axis=-1, keepdims=True)
        outs.append((jnp.dot(p.astype(BF16), vh, preferred_element_type=F32) / den).astype(BF16))
    o = jnp.concatenate(outs, axis=-1)
    o_ref[...] = x + jnp.dot(o, wo_ref[...], preferred_element_type=F32)


def _cross(x, g, mem_k, mem_v, wq, wo, tile, tiles_per_batch):
    n_tok, d = x.shape
    row = lambda i: (i, 0)
    full = lambda i: (0, 0)
    mem = lambda i: (i // tiles_per_batch, 0, 0)
    return pl.pallas_call(
        _cross_kernel, grid=(n_tok // tile,),
        in_specs=[pl.BlockSpec((tile, d), row), pl.BlockSpec((1, d), full),
                  pl.BlockSpec((None, MEM_TOKENS, d), mem), pl.BlockSpec((None, MEM_TOKENS, d), mem),
                  pl.BlockSpec(wq.shape, full), pl.BlockSpec(wo.shape, full)],
        out_specs=pl.BlockSpec((tile, d), row), out_shape=jax.ShapeDtypeStruct((n_tok, d), F32),
        compiler_params=_cparams("parallel"), name="cross")(x, g.reshape(1, d), mem_k, mem_v, wq, wo)


def _oddeven_mergesort_pairs(n):
    pairs = []
    p = 1
    while p < n:
        k = p
        while k >= 1:
            for j in range(k % p, n - k, 2 * k):
                for i in range(min(k, n - j - k)):
                    if (i + j) // (2 * p) == (i + j + k) // (2 * p):
                        pairs.append((i + j, i + j + k))
            k //= 2
        p *= 2
    return pairs


_SORT16 = _oddeven_mergesort_pairs(PEER_TOPK)
PEER_ROWCHUNK = 32


def _peer_route_kernel(x_ref, g_ref, wpq_ref, sk_ref, ht_ref, r1_ref, w1_ref, cnt_ref, w0_ref, qt_s, s_s, sv_s, th_s):
    hn = _rms(x_ref[...], g_ref[...])
    ht = hn.T.astype(BF16)
    ht_ref[...] = ht
    qt_s[...] = jnp.dot(wpq_ref[...], ht, preferred_element_type=F32)
    ngrp = N_KEYS // SUBLANES
    assert ngrp == PEER_TOPK

    def score_one(hc):
        off = pl.multiple_of(hc * PEER_HALF, PEER_HALF)
        qt = qt_s[pl.ds(off, PEER_HALF), :].astype(BF16)
        st = jnp.dot(sk_ref[hc], qt, preferred_element_type=F32)
        s_s[hc] = st
        w = [st[SUBLANES * k:SUBLANES * (k + 1), :] for k in range(ngrp)]
        for a, b in _SORT16:
            w[a], w[b] = jnp.maximum(w[a], w[b]), jnp.minimum(w[a], w[b])
        tops = []
        for r in range(PEER_TOPK):
            mx = jnp.max(w[0], axis=0, keepdims=True)
            tops.append(mx)
            hit = w[0] == mx
            for k in range(PEER_TOPK - 1 - r):
                w[k] = jnp.where(hit, w[k + 1], w[k])
        sv_s[hc] = jnp.concatenate(tops, axis=0)

    def score_body(h, carry):
        score_one(2 * h)
        score_one(2 * h + 1)
        return carry

    lax.fori_loop(0, PEER_HEADS, score_body, 0)

    def head_body(h, carry):
        sv0 = sv_s[2 * h]
        sv1 = sv_s[2 * h + 1]
        half = PEER_TOPK // 2
        pieces = [sv0[0:1] + sv1, sv0[1:2] + sv1[0:half]]
        pieces += [sv0[a:a + 1] + sv1[0:half] for a in range(2, half)]
        pieces += [sv0[half:] + sv1[0:1]]
        cand = jnp.concatenate(pieces, axis=0)
        top = sv0[0:1] + sv1[0:1]
        z = jnp.zeros_like(top)
        tau = top
        for _ in range(PEER_TOPK):
            tau = jnp.max(cand, axis=0, keepdims=True)
            z = z + jnp.exp(tau - top)
            cand = jnp.where(cand == tau, NEG_BIG, cand)
        for b in range(PEER_TOPK):
            sel = (sv0 + sv1[b:b + 1]) >= tau
            th_s[b:b + 1, :] = jnp.min(jnp.where(sel, sv0, -NEG_BIG), axis=0, keepdims=True)
        inv_z = 1.0 / z
        for kk in range(0, N_KEYS, PEER_ROWCHUNK):
            rows = slice(kk, kk + PEER_ROWCHUNK)
            s0 = s_s[2 * h, rows, :]
            s1 = s_s[2 * h + 1, rows, :]
            cnt = jnp.zeros_like(s0)
            rank = jnp.zeros_like(s1)
            for b in range(PEER_TOPK):
                cnt = jnp.where(s0 >= th_s[b:b + 1, :], float(b + 1), cnt)
                rank = jnp.where(sv1[b:b + 1] > s1, float(b + 1), rank)
            cnt_ref[h, rows, :] = cnt
            r1_ref[h, rows, :] = rank.astype(BF16)
            w0_ref[h, rows, :] = jnp.exp(s0 - sv0[0:1])
            w1_ref[h, rows, :] = (jnp.exp(s1 - sv1[0:1]) * inv_z).astype(BF16)
        return carry

    lax.fori_loop(0, PEER_HEADS, head_body, 0)


def _peer_route(x, g, wpq_t, sk, tile):
    n_tok, d = x.shape
    nhc = 2 * PEER_HEADS
    col3 = lambda i: (0, 0, i)
    hshape = (PEER_HEADS, N_KEYS, n_tok)
    hblock = pl.BlockSpec((PEER_HEADS, N_KEYS, tile), col3)
    return pl.pallas_call(
        _peer_route_kernel, grid=(n_tok // tile,),
        in_specs=[pl.BlockSpec((tile, d), lambda i: (i, 0)), pl.BlockSpec((1, d), lambda i: (0, 0)),
                  pl.BlockSpec(wpq_t.shape, lambda i: (0, 0)), pl.BlockSpec(sk.shape, lambda i: (0, 0, 0))],
        out_specs=[pl.BlockSpec((d, tile), lambda i: (0, i)), hblock, hblock, hblock, hblock],
        out_shape=[jax.ShapeDtypeStruct((d, n_tok), BF16),
                   jax.ShapeDtypeStruct(hshape, BF16), jax.ShapeDtypeStruct(hshape, BF16),
                   jax.ShapeDtypeStruct(hshape, F32), jax.ShapeDtypeStruct(hshape, F32)],
        scratch_shapes=[pltpu.VMEM((PEER_HEADS * PEER_QDIM, tile), F32), pltpu.VMEM((nhc, N_KEYS, tile), F32),
                        pltpu.VMEM((nhc, PEER_TOPK, tile), F32), pltpu.VMEM((PEER_TOPK, tile), F32)],
        compiler_params=_cparams("parallel"), name="peer_route")(x, g.reshape(1, d), wpq_t, sk)


def _gelu(z):
    return 0.5 * z * (1.0 + lax.erf(z * (2.0 ** -0.5)))


BF16_ROWS = 2 * SUBLANES


def _peer_dense_kernel(x_ref, ht_ref, r1_ref, w1_ref, cnt_ref, w0_ref, u_ref, vt_ref, gf_ref, y_ref,
                       acc_s, act_s, c_s):
    s = pl.program_id(1)
    nblk = pl.num_programs(1) - 2
    P = x_ref.shape[0]

    @pl.when(s == 0)
    def _():
        acc_s[...] = jnp.zeros_like(acc_s)
        act_s[...] = jnp.zeros_like(act_s)
        c_s[...] = jnp.zeros_like(c_s)

    blk = jnp.clip(s - 1, 0, nblk - 1)
    slab_l = 2 * LANES

    def gate_rows(ii, act_cur, c_cur):
        i = blk * PEER_IBLK + ii
        cnt_rows = [cnt_ref[h, pl.ds(i, 1), :] for h in range(PEER_HEADS)]
        w0_rows = [w0_ref[h, pl.ds(i, 1), :] for h in range(PEER_HEADS)]
        for lg in range(P // slab_l):
            ls = slice(lg * slab_l, (lg + 1) * slab_l)
            cnt_b = [jnp.broadcast_to(r[:, ls], (SUBLANES, slab_l)).astype(BF16) for r in cnt_rows]
            w0_b = [jnp.broadcast_to(r[:, ls], (SUBLANES, slab_l)).astype(BF16) for r in w0_rows]
            for jg in range(N_KEYS // BF16_ROWS):
                halves = []
                for hf in range(2):
                    j0 = jg * BF16_ROWS + hf * SUBLANES
                    js = slice(j0, j0 + SUBLANES)
                    g = None
                    for h in range(PEER_HEADS):
                        w1 = w1_ref[h, js, ls]
                        t = jnp.where(r1_ref[h, js, ls] < cnt_b[h], w1, jnp.zeros_like(w1)) * w0_b[h]
                        g = t if g is None else g + t
                    halves.append(g)
                r0 = ii * N_KEYS + jg * BF16_ROWS
                rows = slice(r0, r0 + BF16_ROWS)
                c_cur[rows, ls] = jnp.concatenate(halves, axis=0) * act_cur[rows, ls]

    cur = lax.rem(s + 1, 2)
    prev = lax.rem(s, 2)
    acc_s[...] += jnp.dot(vt_ref[...], c_s[prev], preferred_element_type=F32)
    for ii in range(PEER_IBLK):
        gate_rows(ii, act_s.at[cur], c_s.at[cur])
    act_s[prev] = _gelu(jnp.dot(u_ref[...], ht_ref[...], preferred_element_type=F32)).astype(BF16)

    @pl.when(s == nblk + 1)
    def _():
        y = x_ref[...] + acc_s[...].T
        y_ref[...] = _rms(y, gf_ref[...])


def _peer_dense(x, ht, r1, w1, cnt, w0, u, vt, g_final, tile):
    n_tok, d = x.shape
    n_exp = u.shape[0]
    nblk = n_exp // PEER_EBLK
    hblock = pl.BlockSpec((PEER_HEADS, N_KEYS, tile), lambda t, s: (0, 0, t))
    return pl.pallas_call(
        _peer_dense_kernel, grid=(n_tok // tile, nblk + 2),
        in_specs=[pl.BlockSpec((tile, d), lambda t, s: (t, 0)),
                  pl.BlockSpec((d, tile), lambda t, s: (0, t)),
                  hblock, hblock, hblock, hblock,
                  pl.BlockSpec((PEER_EBLK, d), lambda t, s: (jnp.minimum(s, nblk - 1), 0)),
                  pl.BlockSpec((d, PEER_EBLK), lambda t, s: (0, jnp.clip(s - 2, 0, nblk - 1))),
                  pl.BlockSpec((1, d), lambda t, s: (0, 0))],
        out_specs=pl.BlockSpec((tile, d), lambda t, s: (t, 0)),
        out_shape=jax.ShapeDtypeStruct((n_tok, d), F32),
        scratch_shapes=[pltpu.VMEM((d, tile), F32), pltpu.VMEM((2, PEER_EBLK, tile), BF16),
                        pltpu.VMEM((2, PEER_EBLK, tile), BF16)],
        compiler_params=_cparams("parallel", "arbitrary"), name="peer_dense")(
            x, ht, r1, w1, cnt, w0, u, vt, g_final.reshape(1, d))


def _pad_rows(a, rows):
    return jnp.pad(a, ((0, 0), (rows - a.shape[1], 0), (0, 0)))


def _layer(x3, mem_k, mem_v, a_prev, b_prev, wts, tile):
    bsz, t, d = x3.shape
    n_tok = bsz * t
    x = x3.reshape(n_tok, d)
    za, zb, zif, zg = _norm_proj(x, wts["g_mix"], [wts["w_a"], wts["w_b"], wts["w_if"], wts["w_g"]], tile)

    if a_prev is None:
        band = (BAND_CHUNKS + 1) * CHUNK
        q_pos = np.arange(CHUNK) + BAND_CHUNKS * CHUNK
        bias = _pair_bias(wts["rel_bias"], q_pos, np.arange(band))
        out_a = _band_prompt(za, bias, bsz, t)
        L = CHUNK
        conv_prev = jnp.zeros((bsz, SUBLANES, 2 * B_WIDTH), F32)
        c0 = jnp.zeros((bsz, B_HEADS, B_HEAD_DIM, B_HEAD_DIM), F32)
        n0 = jnp.zeros((bsz, SUBLANES, B_HEAD_DIM), F32)
        m0 = jnp.zeros((bsz, SUBLANES, LANES), F32)
    else:
        k_prev, v_prev = a_prev
        past = k_prev.shape[1]
        za3 = za.reshape(bsz, t, 3 * A_WIDTH)
        k_all = jnp.concatenate([k_prev.reshape(bsz, past, A_WIDTH), za3[:, :, A_WIDTH:2 * A_WIDTH]], axis=1)
        v_all = jnp.concatenate([v_prev.reshape(bsz, past, A_WIDTH), za3[:, :, 2 * A_WIDTH:]], axis=1)
        bias = _pair_bias(wts["rel_bias"], past + np.arange(t), np.arange(past + t))
        out_a = _band_cached(za, k_all, v_all, bias, bsz, t)
        L = t
        conv_prev, c0, n0, m0 = b_prev
        conv_prev = _pad_rows(conv_prev, SUBLANES)
        n0 = jnp.pad(n0, ((0, 0), (0, SUBLANES - B_HEADS), (0, 0)))
        m0 = jnp.broadcast_to(jnp.pad(m0, ((0, 0), (0, SUBLANES - B_HEADS)))[:, :, None], (bsz, SUBLANES, LANES))

    hb, c_new, n_new, m_new = _mlstm(zb, zif, wts["conv_w"], wts["conv_b"], wts["b_if"], wts["g_head"],
                                     conv_prev, c0, n0, m0, bsz, t, L)

    x1 = _merge(x, out_a, hb, zg, wts["w_a_up"], wts["w_b_up"], wts["w_out"], tile)
    x2 = _cross(x1, wts["g_cross"], mem_k, mem_v, wts["w_cq"], wts["w_co"], min(tile, t), t // min(tile, t))

    ht, r1, w1, cnt, w0 = _peer_route(x2, wts["g_ffn"], wts["w_pq_t"], wts["sub_keys"], PEER_TOK)
    y = _peer_dense(x2, ht, r1, w1, cnt, w0, wts["peer_u"], wts["peer_v_t"], wts["g_final"], PEER_TOK)

    za3 = za.reshape(bsz, t, 3 * A_WIDTH)
    keep = min(BAND_CHUNKS * CHUNK, t) if a_prev is None else t
    new_ak = za3[:, t - keep:, A_WIDTH:2 * A_WIDTH].reshape(bsz, keep, A_HEADS, A_HEAD_DIM)
    new_av = za3[:, t - keep:, 2 * A_WIDTH:].reshape(bsz, keep, A_HEADS, A_HEAD_DIM)
    new_conv = zb.reshape(bsz, t, -1)[:, t - (CONV_W - 1):, :2 * B_WIDTH]
    if t < CONV_W - 1:
        raise NotImplementedError("fewer new frames than the conv history")
    return (y.reshape(bsz, t, d),
            (new_ak, new_av, new_conv, c_new, n_new[:, :B_HEADS, :], m_new[:, :B_HEADS, 0]))


def kernel(x_prompt, x_sample, mem_prompt, cache_a_k, cache_a_v, state_b_conv, state_b_C, state_b_n, state_b_m, cache_mem_k, cache_mem_v, g_mix, w_in, conv_w, conv_b, b_if, g_head, rel_bias, w_a_up, w_b_up, w_out, g_mem, w_mk, w_mv, g_cross, w_cq, w_co, g_ffn, w_pq, sub_keys, peer_u, peer_v, g_final):
    depth = w_in.shape[0]
    assert depth == 1, "the final norm is fused into the last PEER call; one layer supported"
    l = 0
    bsz_p, t_p, d = x_prompt.shape
    bsz_s, t_s, _ = x_sample.shape

    wi = w_in[l]
    o_b = 3 * A_WIDTH
    o_if = o_b + 4 * B_WIDTH
    o_g = o_if + 2 * B_HEADS
    wts = {
        "g_mix": g_mix[l],
        "w_a": wi[:, :o_b].astype(BF16),
        "w_b": wi[:, o_b:o_if].astype(BF16),
        "w_if": jnp.pad(wi[:, o_if:o_g], ((0, 0), (0, IF_PAD - 2 * B_HEADS))).astype(BF16),
        "w_g": wi[:, o_g:].astype(BF16),
        "conv_w": conv_w[l], "conv_b": conv_b[l].reshape(1, -1),
        "b_if": jnp.pad(b_if[l], (0, IF_PAD - 2 * B_HEADS)).reshape(1, IF_PAD),
        "g_head": g_head[l].reshape(1, -1),
        "rel_bias": rel_bias[l],
        "w_a_up": w_a_up[l].astype(BF16), "w_b_up": w_b_up[l].astype(BF16), "w_out": w_out[l].astype(BF16),
        "g_cross": g_cross[l], "w_cq": w_cq[l].astype(BF16), "w_co": w_co[l].astype(BF16),
        "g_ffn": g_ffn[l],
        "w_pq_t": w_pq[l].T.astype(BF16),
        "sub_keys": sub_keys[l].reshape(2 * PEER_HEADS, N_KEYS, PEER_HALF).astype(BF16),
        "peer_u": peer_u[l].astype(BF16),
        "peer_v_t": peer_v[l].T.astype(BF16),
        "g_final": g_final,
    }

    mk_flat, mv_flat = _norm_proj(mem_prompt.reshape(bsz_p * MEM_TOKENS, d), g_mem[l],
                                  [w_mk[l].astype(BF16), w_mv[l].astype(BF16)], 256)
    mk_p = mk_flat.reshape(bsz_p, MEM_TOKENS, d)
    mv_p = mv_flat.reshape(bsz_p, MEM_TOKENS, d)

    yp, sp = _layer(x_prompt, mk_p, mv_p, None, None, wts, 256)
    ys, sq = _layer(x_sample, cache_mem_k[l].reshape(bsz_s, MEM_TOKENS, d), cache_mem_v[l].reshape(bsz_s, MEM_TOKENS, d),
                    (cache_a_k[l], cache_a_v[l]), (state_b_conv[l], state_b_C[l], state_b_n[l], state_b_m[l]), wts, 256)

    ps = sp + (mk_p.reshape(bsz_p, MEM_TOKENS, X_HEADS, X_HEAD_DIM), mv_p.reshape(bsz_p, MEM_TOKENS, X_HEADS, X_HEAD_DIM))
    return (yp, ys) + tuple(a[None] for a in ps) + tuple(a[None] for a in sq)
```

```python
import numpy as np
import jax
import jax.numpy as jnp
from jax import lax
from jax.experimental import pallas as pl
from jax.experimental.pallas import tpu as pltpu

F32 = jnp.float32
BF16 = jnp.bfloat16

D_MODEL = 1024
CHUNK = 64
A_HEADS = 8
A_HEAD_DIM = 64
A_WIDTH = A_HEADS * A_HEAD_DIM
BAND_CHUNKS = 8
MAX_REL = 128
B_HEADS = 4
B_HEAD_DIM = 128
B_WIDTH = B_HEADS * B_HEAD_DIM
CONV_W = 4
MEM_TOKENS = 256
X_HEADS = 4
X_HEAD_DIM = D_MODEL // X_HEADS
PEER_HEADS = 8
N_KEYS = 128
PEER_TOPK = 16
PEER_QDIM = 256
PEER_HALF = PEER_QDIM // 2
EPS = 1e-6

LANES = 128
SUBLANES = 8
VMEM_LIMIT = 56 * 1024 * 1024
NEG_BIG = -1e30

A_PAIRS = A_HEADS // 2
IF_PAD = LANES
PEER_TOK = 512
PEER_EBLK = 512
PEER_IBLK = PEER_EBLK // N_KEYS


def _cparams(*sem):
    return pltpu.CompilerParams(dimension_semantics=sem, vmem_limit_bytes=VMEM_LIMIT)


def _rms(xf, g):
    r = lax.rsqrt(jnp.mean(xf * xf, axis=-1, keepdims=True) + EPS)
    return xf * r * g


def _sigmoid(x):
    return 1.0 / (1.0 + jnp.exp(-x))


def _log_sigmoid(x):
    return jnp.minimum(x, 0.0) - jnp.log1p(jnp.exp(-jnp.abs(x)))


def _norm_proj_kernel(x_ref, g_ref, *refs):
    n = len(refs) // 2
    h = _rms(x_ref[...], g_ref[...]).astype(BF16)
    for w_ref, o_ref in zip(refs[:n], refs[n:]):
        o_ref[...] = jnp.dot(h, w_ref[...], preferred_element_type=F32)


def _norm_proj(x, g, ws, tile):
    n_tok, d = x.shape
    assert n_tok % tile == 0
    in_specs = [pl.BlockSpec((tile, d), lambda i: (i, 0)), pl.BlockSpec((1, d), lambda i: (0, 0))]
    in_specs += [pl.BlockSpec(w.shape, lambda i: (0, 0)) for w in ws]
    out_specs = [pl.BlockSpec((tile, w.shape[1]), lambda i: (i, 0)) for w in ws]
    out_shape = [jax.ShapeDtypeStruct((n_tok, w.shape[1]), F32) for w in ws]
    return pl.pallas_call(
        _norm_proj_kernel, grid=(n_tok // tile,), in_specs=in_specs, out_specs=out_specs, out_shape=out_shape,
        compiler_params=_cparams("parallel"), name="norm_proj")(x, g.reshape(1, d), *ws)


def _attn_pair(qp, kb, vb, bias, valid_from):
    m_rows = qp.shape[0]
    lane = lax.broadcasted_iota(jnp.int32, qp.shape, 1)
    q0 = jnp.where(lane < A_HEAD_DIM, qp, 0.0)
    q1 = jnp.where(lane >= A_HEAD_DIM, qp, 0.0)
    qs = jnp.concatenate([q0, q1], axis=0).astype(BF16)
    s = lax.dot_general(qs, kb, (((1,), (1,)), ((), ())), preferred_element_type=F32) + bias
    if valid_from is not None:
        col = lax.broadcasted_iota(jnp.int32, s.shape, 1)
        s = jnp.where(col >= valid_from, s, NEG_BIG)
    mx = jnp.max(s, axis=-1, keepdims=True)
    p = jnp.exp(s - mx)
    den = jnp.sum(p, axis=-1, keepdims=True)
    o = jnp.dot(p.astype(BF16), vb, preferred_element_type=F32) / den
    return jnp.where(lane < A_HEAD_DIM, o[:m_rows], o[m_rows:])


def _band_prompt_kernel(q_ref, k_ref, v_ref, bias_ref, o_ref, kpad, vpad):
    c = pl.program_id(1)
    pad = BAND_CHUNKS * CHUNK
    width = pad + CHUNK

    @pl.when(c == 0)
    def _():
        kpad[0:pad, :] = jnp.zeros((pad, A_WIDTH), BF16)
        vpad[0:pad, :] = jnp.zeros((pad, A_WIDTH), BF16)
        kpad[pad:, :] = k_ref[...].astype(BF16)
        vpad[pad:, :] = v_ref[...].astype(BF16)

    for k in range(q_ref.shape[0] // CHUNK):
        cc = c * (q_ref.shape[0] // CHUNK) + k
        rows = slice(k * CHUNK, (k + 1) * CHUNK)
        start = pl.multiple_of(cc * CHUNK, CHUNK)
        valid_from = pad - cc * CHUNK
        for hp in range(A_PAIRS):
            cols = slice(hp * LANES, (hp + 1) * LANES)
            qp = q_ref[rows, cols] * (A_HEAD_DIM ** -0.5)
            kb = kpad[pl.ds(start, width), cols]
            vb = vpad[pl.ds(start, width), cols]
            o_ref[rows, cols] = _attn_pair(qp, kb, vb, bias_ref[hp], valid_from)


BAND_CHUNKS_PER_STEP = 4


def _band_prompt(za, bias, bsz, t):
    per = BAND_CHUNKS_PER_STEP if (t // CHUNK) % BAND_CHUNKS_PER_STEP == 0 else 1
    nc = t // (CHUNK * per)
    pad = BAND_CHUNKS * CHUNK
    return pl.pallas_call(
        _band_prompt_kernel, grid=(bsz, nc),
        in_specs=[pl.BlockSpec((CHUNK * per, A_WIDTH), lambda b, c: (b * nc + c, 0)),
                  pl.BlockSpec((t, A_WIDTH), lambda b, c: (b, 1)),
                  pl.BlockSpec((t, A_WIDTH), lambda b, c: (b, 2)),
                  pl.BlockSpec(bias.shape, lambda b, c: (0, 0, 0))],
        out_specs=pl.BlockSpec((CHUNK * per, A_WIDTH), lambda b, c: (b * nc + c, 0)),
        out_shape=jax.ShapeDtypeStruct((bsz * t, A_WIDTH), F32),
        scratch_shapes=[pltpu.VMEM((pad + t, A_WIDTH), BF16), pltpu.VMEM((pad + t, A_WIDTH), BF16)],
        compiler_params=_cparams("parallel", "arbitrary"), name="band_prompt")(za, za, za, bias)


def _band_cached_kernel(q_ref, k_ref, v_ref, bias_ref, o_ref):
    for hp in range(A_PAIRS):
        cols = slice(hp * LANES, (hp + 1) * LANES)
        qp = q_ref[:, cols] * (A_HEAD_DIM ** -0.5)
        kb = k_ref[:, cols].astype(BF16)
        vb = v_ref[:, cols].astype(BF16)
        o_ref[:, cols] = _attn_pair(qp, kb, vb, bias_ref[hp], None)


def _band_cached(za, k_all, v_all, bias, bsz, s_len):
    lk = k_all.shape[1]
    return pl.pallas_call(
        _band_cached_kernel, grid=(bsz,),
        in_specs=[pl.BlockSpec((s_len, A_WIDTH), lambda b: (b, 0)),
                  pl.BlockSpec((None, lk, A_WIDTH), lambda b: (b, 0, 0)),
                  pl.BlockSpec((None, lk, A_WIDTH), lambda b: (b, 0, 0)),
                  pl.BlockSpec(bias.shape, lambda b: (0, 0, 0))],
        out_specs=pl.BlockSpec((s_len, A_WIDTH), lambda b: (b, 0)),
        out_shape=jax.ShapeDtypeStruct((bsz * s_len, A_WIDTH), F32),
        compiler_params=_cparams("parallel"), name="band_cached")(za, k_all, v_all, bias)


def _pair_bias(rel_bias, q_pos, k_pos):
    nq, nk = q_pos.shape[0], k_pos.shape[0]
    assert np.all(np.diff(q_pos) == 1) and np.all(np.diff(k_pos) == 1), "Toeplitz layout needs consecutive positions"
    d_max = int(q_pos[-1] - k_pos[0])
    idx = np.clip(d_max - np.arange(nq + nk - 1), -MAX_REL, MAX_REL) + MAX_REL
    u = jnp.take(rel_bias, jnp.asarray(idx, jnp.int32), axis=1)
    full = jnp.stack([lax.slice_in_dim(u, nq - 1 - q, nq - 1 - q + nk, axis=1) for q in range(nq)], axis=1)
    return full.reshape(A_PAIRS, 2 * nq, nk).astype(F32)


def _mlstm_kernel(zb_all, zif_all, convw_ref, convb_ref, bif_ref, ghead_ref, cprev_ref, c0_ref, n0_ref, m0_ref,
                  hb_all, c_out, n_out, m_out, c_all, n_all, m_all, ubuf_all):
    c = pl.program_id(1)
    nchunks = pl.num_programs(1)
    nseq, L = zif_all.shape[0], zif_all.shape[1]
    hist = SUBLANES

    @pl.when(c == 0)
    def _():
        c_all[...] = c0_ref[...]
        n_all[...] = n0_ref[...]
        m_all[...] = m0_ref[...]
        ubuf_all[:, 0:hist, :] = cprev_ref[...]

    _mlstm_group(zb_all, zif_all, convw_ref, convb_ref, bif_ref, ghead_ref, hb_all, c_all, n_all, m_all, ubuf_all, L)

    @pl.when(c == nchunks - 1)
    def _():
        c_out[...] = c_all[...]
        n_out[...] = n_all[...]
        m_out[...] = m_all[...]


def _mlstm_group(zb_all, zif_all, convw_ref, convb_ref, bif_ref, ghead_ref, hb_all, c_all, n_all, m_all, ubuf_all, L):
    nseq = zif_all.shape[0]
    qk_w = 2 * B_WIDTH
    hist = SUBLANES
    hi = lax.Precision.HIGHEST
    eye = (lax.broadcasted_iota(jnp.int32, (LANES, LANES), 0) ==
           lax.broadcasted_iota(jnp.int32, (LANES, LANES), 1)).astype(F32)
    row = lax.broadcasted_iota(jnp.int32, (L, L), 0)
    colm = lax.broadcasted_iota(jnp.int32, (L, L), 1)
    causal = colm <= row
    lower = causal.astype(F32)
    upper = (row <= colm).astype(F32)

    qk, gates = [], []
    for bb in range(nseq):
        ubuf = ubuf_all.at[bb]
        ubuf[hist:hist + L, :] = zb_all[bb, :, 0:qk_w]
        acc = jnp.broadcast_to(convb_ref[...], (L, qk_w))
        for j in range(CONV_W):
            off = hist - (CONV_W - 1) + j
            acc = acc + ubuf[off:off + L, :] * convw_ref[j:j + 1, :]
        ubuf[0:hist, :] = ubuf[L:L + hist, :]
        qk.append(acc * _sigmoid(acc))
        gates.append(zif_all[bb] + bif_ref[...])
    gates_t = [lax.dot_general(eye, g, (((1,), (1,)), ((), ())), precision=hi, preferred_element_type=F32)
               for g in gates]
    b_col_all = [jnp.dot(lower, _log_sigmoid(g), precision=hi, preferred_element_type=F32) for g in gates]
    b_row_all = [jnp.dot(_log_sigmoid(gt), upper, precision=hi, preferred_element_type=F32) for gt in gates_t]

    units = [(bb, h) for bb in range(nseq) for h in range(B_HEADS)]
    st = {}
    for u in units:
        bb, h = u
        d = {}
        d["qh"] = qk[bb][:, h * B_HEAD_DIM:(h + 1) * B_HEAD_DIM]
        d["kh"] = qk[bb][:, B_WIDTH + h * B_HEAD_DIM:B_WIDTH + (h + 1) * B_HEAD_DIM] * (B_HEAD_DIM ** -0.5)
        vh = zb_all[bb, :, qk_w + h * B_HEAD_DIM:qk_w + (h + 1) * B_HEAD_DIM]
        d["qb"], d["kb"], d["vb"] = d["qh"].astype(BF16), d["kh"].astype(BF16), vh.astype(BF16)
        d["ig_col"] = gates[bb][:, h:h + 1]
        ig_row = gates_t[bb][h:h + 1, :]
        d["b_col"] = b_col_all[bb][:, B_HEADS + h:B_HEADS + h + 1]
        b_row = b_row_all[bb][B_HEADS + h:B_HEADS + h + 1, :]
        d["m0"] = m_all[bb, h:h + 1, 0:1]
        d["c0"] = c_all[bb, h]
        d["n0"] = n_all[bb, h:h + 1, :]
        dm = jnp.where(causal, d["b_col"] - b_row + ig_row, NEG_BIG)
        inter = d["b_col"] + d["m0"]
        d["m"] = jnp.maximum(jnp.max(dm, axis=-1, keepdims=True), inter)
        d["w_intra"] = jnp.exp(dm - d["m"])
        d["w_inter"] = jnp.exp(inter - d["m"])
        st[u] = d
    for u in units:
        d = st[u]
        d["a"] = lax.dot_general(d["qb"], d["kb"], (((1,), (1,)), ((), ())),
                                 preferred_element_type=F32) * d["w_intra"]
    for u in units:
        d = st[u]
        num = (jnp.dot(d["a"].astype(BF16), d["vb"], preferred_element_type=F32) +
               d["w_inter"] * jnp.dot(d["qb"], d["c0"].astype(BF16), preferred_element_type=F32))
        den = (jnp.sum(d["a"], axis=-1, keepdims=True) +
               d["w_inter"] * jnp.sum(d["qh"] * d["n0"], axis=-1, keepdims=True))
        d["hh"] = num / jnp.maximum(jnp.abs(den), jnp.exp(-d["m"]))
    for u in units:
        bb, h = u
        d = st[u]
        bl = d["b_col"][L - 1:L, :]
        wk = bl - d["b_col"] + d["ig_col"]
        ml = jnp.maximum(bl + d["m0"], jnp.max(wk, axis=0, keepdims=True))
        a0 = jnp.exp(bl + d["m0"] - ml)
        ws = jnp.exp(wk - ml)
        kw = d["kh"] * ws
        c_all[bb, h] = a0 * d["c0"] + lax.dot_general(kw.astype(BF16), d["vb"], (((0,), (0,)), ((), ())),
                                                      preferred_element_type=F32)
        n_all[bb, h:h + 1, :] = a0 * d["n0"] + jnp.sum(kw, axis=0, keepdims=True)
        m_all[bb, h:h + 1, :] = jnp.broadcast_to(ml, (1, LANES))
    for u in units:
        bb, h = u
        cols = slice(h * B_HEAD_DIM, (h + 1) * B_HEAD_DIM)
        oh = zb_all[bb, :, qk_w + B_WIDTH + h * B_HEAD_DIM:qk_w + B_WIDTH + (h + 1) * B_HEAD_DIM]
        hn = _rms(st[u]["hh"], ghead_ref[:, cols])
        hb_all[bb, :, cols] = _sigmoid(oh) * hn


MLSTM_SEQ_PER_STEP = 4


def _mlstm(zb, zif, conv_w, conv_b, bif, g_head, conv_prev, c0, n0, m0, bsz, t, L):
    nc = t // L
    qk_w = 2 * B_WIDTH
    hd = B_HEAD_DIM
    g = next(k for k in (MLSTM_SEQ_PER_STEP, 2, 1) if bsz % k == 0)
    full2 = lambda b, c: (0, 0)
    seq3 = lambda b, c: (b, c, 0)
    st3 = lambda b, c: (b, 0, 0)
    outs = pl.pallas_call(
        _mlstm_kernel, grid=(bsz // g, nc),
        in_specs=[pl.BlockSpec((g, L, zb.shape[1]), seq3),
                  pl.BlockSpec((g, L, IF_PAD), seq3),
                  pl.BlockSpec((CONV_W, qk_w), full2),
                  pl.BlockSpec((1, qk_w), full2),
                  pl.BlockSpec((1, IF_PAD), full2),
                  pl.BlockSpec((1, B_WIDTH), full2),
                  pl.BlockSpec((g, SUBLANES, qk_w), st3),
                  pl.BlockSpec((g, B_HEADS, hd, hd), lambda b, c: (b, 0, 0, 0)),
                  pl.BlockSpec((g, SUBLANES, hd), st3),
                  pl.BlockSpec((g, SUBLANES, LANES), st3)],
        out_specs=[pl.BlockSpec((g, L, B_WIDTH), seq3),
                   pl.BlockSpec((g, B_HEADS, hd, hd), lambda b, c: (b, 0, 0, 0)),
                   pl.BlockSpec((g, SUBLANES, hd), st3),
                   pl.BlockSpec((g, SUBLANES, LANES), st3)],
        out_shape=[jax.ShapeDtypeStruct((bsz, t, B_WIDTH), F32),
                   jax.ShapeDtypeStruct((bsz, B_HEADS, hd, hd), F32),
                   jax.ShapeDtypeStruct((bsz, SUBLANES, hd), F32),
                   jax.ShapeDtypeStruct((bsz, SUBLANES, LANES), F32)],
        scratch_shapes=[pltpu.VMEM((g, B_HEADS, hd, hd), F32), pltpu.VMEM((g, SUBLANES, hd), F32),
                        pltpu.VMEM((g, SUBLANES, LANES), F32), pltpu.VMEM((g, SUBLANES + L, qk_w), F32)],
        compiler_params=_cparams("parallel", "arbitrary"), name="mlstm")(
            zb.reshape(bsz, t, -1), zif.reshape(bsz, t, -1), conv_w, conv_b, bif, g_head, conv_prev, c0, n0, m0)
    return (outs[0].reshape(bsz * t, B_WIDTH),) + tuple(outs[1:])


def _merge_kernel(x_ref, oa_ref, hb_ref, zg_ref, wa_ref, wb_ref, wo_ref, o_ref):
    ua = jnp.dot(oa_ref[...].astype(BF16), wa_ref[...], preferred_element_type=F32)
    ub = jnp.dot(hb_ref[...].astype(BF16), wb_ref[...], preferred_element_type=F32)
    mixed = _sigmoid(zg_ref[:, 0:D_MODEL]) * ua + _sigmoid(zg_ref[:, D_MODEL:]) * ub
    o_ref[...] = x_ref[...] + jnp.dot(mixed.astype(BF16), wo_ref[...], preferred_element_type=F32)


def _merge(x, oa, hb, zg, wa, wb, wo, tile):
    n_tok, d = x.shape
    row = lambda i: (i, 0)
    full = lambda i: (0, 0)
    return pl.pallas_call(
        _merge_kernel, grid=(n_tok // tile,),
        in_specs=[pl.BlockSpec((tile, d), row), pl.BlockSpec((tile, A_WIDTH), row), pl.BlockSpec((tile, B_WIDTH), row),
                  pl.BlockSpec((tile, 2 * d), row), pl.BlockSpec(wa.shape, full), pl.BlockSpec(wb.shape, full),
                  pl.BlockSpec(wo.shape, full)],
        out_specs=pl.BlockSpec((tile, d), row), out_shape=jax.ShapeDtypeStruct((n_tok, d), F32),
        compiler_params=_cparams("parallel"), name="merge")(x, oa, hb, zg, wa, wb, wo)


def _cross_kernel(x_ref, g_ref, k_ref, v_ref, wq_ref, wo_ref, o_ref):
    x = x_ref[...]
    hn = _rms(x, g_ref[...]).astype(BF16)
    q = jnp.dot(hn, wq_ref[...], preferred_element_type=F32) * (X_HEAD_DIM ** -0.5)
    outs = []
    for h in range(X_HEADS):
        cols = slice(h * X_HEAD_DIM, (h + 1) * X_HEAD_DIM)
        qh = q[:, cols].astype(BF16)
        kh = k_ref[:, cols].astype(BF16)
        vh = v_ref[:, cols].astype(BF16)
        s = lax.dot_general(qh, kh, (((1,), (1,)), ((), ())), preferred_element_type=F32)
        mx = jnp.max(s, axis=-1, keepdims=True)
        p = jnp.exp(s - mx)
        den = jnp.sum(p, axis=-1, keepdims=True)
        outs.append((jnp.dot(p.astype(BF16), vh, preferred_element_type=F32) / den).astype(BF16))
    o = jnp.concatenate(outs, axis=-1)
    o_ref[...] = x + jnp.dot(o, wo_ref[...], preferred_element_type=F32)


def _cross(x, g, mem_k, mem_v, wq, wo, tile, tiles_per_batch):
    n_tok, d = x.shape
    row = lambda i: (i, 0)
    full = lambda i: (0, 0)
    mem = lambda i: (i // tiles_per_batch, 0, 0)
    return pl.pallas_call(
        _cross_kernel, grid=(n_tok // tile,),
        in_specs=[pl.BlockSpec((tile, d), row), pl.BlockSpec((1, d), full),
                  pl.BlockSpec((None, MEM_TOKENS, d), mem), pl.BlockSpec((None, MEM_TOKENS, d), mem),
                  pl.BlockSpec(wq.shape, full), pl.BlockSpec(wo.shape, full)],
        out_specs=pl.BlockSpec((tile, d), row), out_shape=jax.ShapeDtypeStruct((n_tok, d), F32),
        compiler_params=_cparams("parallel"), name="cross")(x, g.reshape(1, d), mem_k, mem_v, wq, wo)


def _oddeven_mergesort_pairs(n):
    pairs = []
    p = 1
    while p < n:
        k = p
        while k >= 1:
            for j in range(k % p, n - k, 2 * k):
                for i in range(min(k, n - j - k)):
                    if (i + j) // (2 * p) == (i + j + k) // (2 * p):
                        pairs.append((i + j, i + j + k))
            k //= 2
        p *= 2
    return pairs


_SORT16 = _oddeven_mergesort_pairs(PEER_TOPK)
PEER_ROWCHUNK = 32


def _peer_route_kernel(x_ref, g_ref, wpq_ref, sk_ref, ht_ref, r1_ref, w1_ref, cnt_ref, w0_ref, qt_s, s_s, sv_s, th_s):
    hn = _rms(x_ref[...], g_ref[...])
    ht = hn.T.astype(BF16)
    ht_ref[...] = ht
    qt_s[...] = jnp.dot(wpq_ref[...], ht, preferred_element_type=F32)
    ngrp = N_KEYS // SUBLANES
    assert ngrp == PEER_TOPK

    def score_one(hc):
        off = pl.multiple_of(hc * PEER_HALF, PEER_HALF)
        qt = qt_s[pl.ds(off, PEER_HALF), :].astype(BF16)
        st = jnp.dot(sk_ref[hc], qt, preferred_element_type=F32)
        s_s[hc] = st
        w = [st[SUBLANES * k:SUBLANES * (k + 1), :] for k in range(ngrp)]
        for a, b in _SORT16:
            w[a], w[b] = jnp.maximum(w[a], w[b]), jnp.minimum(w[a], w[b])
        tops = []
        for r in range(PEER_TOPK):
            mx = jnp.max(w[0], axis=0, keepdims=True)
            tops.append(mx)
            hit = w[0] == mx
            for k in range(PEER_TOPK - 1 - r):
                w[k] = jnp.where(hit, w[k + 1], w[k])
        sv_s[hc] = jnp.concatenate(tops, axis=0)

    def score_body(h, carry):
        score_one(2 * h)
        score_one(2 * h + 1)
        return carry

    lax.fori_loop(0, PEER_HEADS, score_body, 0)

    def head_body(h, carry):
        sv0 = sv_s[2 * h]
        sv1 = sv_s[2 * h + 1]
        half = PEER_TOPK // 2
        pieces = [sv0[0:1] + sv1, sv0[1:2] + sv1[0:half]]
        pieces += [sv0[a:a + 1] + sv1[0:half] for a in range(2, half)]
        pieces += [sv0[half:] + sv1[0:1]]
        cand = jnp.concatenate(pieces, axis=0)
        top = sv0[0:1] + sv1[0:1]
        z = jnp.zeros_like(top)
        tau = top
        for _ in range(PEER_TOPK):
            tau = jnp.max(cand, axis=0, keepdims=True)
            z = z + jnp.exp(tau - top)
            cand = jnp.where(cand == tau, NEG_BIG, cand)
        for b in range(PEER_TOPK):
            sel = (sv0 + sv1[b:b + 1]) >= tau
            th_s[b:b + 1, :] = jnp.min(jnp.where(sel, sv0, -NEG_BIG), axis=0, keepdims=True)
        inv_z = 1.0 / z
        for kk in range(0, N_KEYS, PEER_ROWCHUNK):
            rows = slice(kk, kk + PEER_ROWCHUNK)
            s0 = s_s[2 * h, rows, :]
            s1 = s_s[2 * h + 1, rows, :]
            cnt = jnp.zeros_like(s0)
            rank = jnp.zeros_like(s1)
            for b in range(PEER_TOPK):
                cnt = jnp.where(s0 >= th_s[b:b + 1, :], float(b + 1), cnt)
                rank = jnp.where(sv1[b:b + 1] > s1, float(b + 1), rank)
            cnt_ref[h, rows, :] = cnt
            r1_ref[h, rows, :] = rank.astype(BF16)
            w0_ref[h, rows, :] = jnp.exp(s0 - sv0[0:1])
            w1_ref[h, rows, :] = (jnp.exp(s1 - sv1[0:1]) * inv_z).astype(BF16)
        return carry

    lax.fori_loop(0, PEER_HEADS, head_body, 0)


def _peer_route(x, g, wpq_t, sk, tile):
    n_tok, d = x.shape
    nhc = 2 * PEER_HEADS
    col3 = lambda i: (0, 0, i)
    hshape = (PEER_HEADS, N_KEYS, n_tok)
    hblock = pl.BlockSpec((PEER_HEADS, N_KEYS, tile), col3)
    return pl.pallas_call(
        _peer_route_kernel, grid=(n_tok // tile,),
        in_specs=[pl.BlockSpec((tile, d), lambda i: (i, 0)), pl.BlockSpec((1, d), lambda i: (0, 0)),
                  pl.BlockSpec(wpq_t.shape, lambda i: (0, 0)), pl.BlockSpec(sk.shape, lambda i: (0, 0, 0))],
        out_specs=[pl.BlockSpec((d, tile), lambda i: (0, i)), hblock, hblock, hblock, hblock],
        out_shape=[jax.ShapeDtypeStruct((d, n_tok), BF16),
                   jax.ShapeDtypeStruct(hshape, BF16), jax.ShapeDtypeStruct(hshape, BF16),
                   jax.ShapeDtypeStruct(hshape, F32), jax.ShapeDtypeStruct(hshape, F32)],
        scratch_shapes=[pltpu.VMEM((PEER_HEADS * PEER_QDIM, tile), F32), pltpu.VMEM((nhc, N_KEYS, tile), F32),
                        pltpu.VMEM((nhc, PEER_TOPK, tile), F32), pltpu.VMEM((PEER_TOPK, tile), F32)],
        compiler_params=_cparams("parallel"), name="peer_route")(x, g.reshape(1, d), wpq_t, sk)


def _gelu(z):
    return 0.5 * z * (1.0 + lax.erf(z * (2.0 ** -0.5)))


BF16_ROWS = 2 * SUBLANES


def _peer_dense_kernel(x_ref, ht_ref, r1_ref, w1_ref, cnt_ref, w0_ref, u_ref, vt_ref, gf_ref, y_ref,
                       acc_s, act_s, c_s):
    s = pl.program_id(1)
    nblk = pl.num_programs(1) - 2
    P = x_ref.shape[0]

    @pl.when(s == 0)
    def _():
        acc_s[...] = jnp.zeros_like(acc_s)
        act_s[...] = jnp.zeros_like(act_s)
        c_s[...] = jnp.zeros_like(c_s)

    blk = jnp.clip(s - 1, 0, nblk - 1)
    slab_l = 2 * LANES

    def gate_rows(ii, act_cur, c_cur):
        i = blk * PEER_IBLK + ii
        cnt_rows = [cnt_ref[h, pl.ds(i, 1), :] for h in range(PEER_HEADS)]
        w0_rows = [w0_ref[h, pl.ds(i, 1), :] for h in range(PEER_HEADS)]
        for lg in range(P // slab_l):
            ls = slice(lg * slab_l, (lg + 1) * slab_l)
            cnt_b = [jnp.broadcast_to(r[:, ls], (SUBLANES, slab_l)).astype(BF16) for r in cnt_rows]
            w0_b = [jnp.broadcast_to(r[:, ls], (SUBLANES, slab_l)).astype(BF16) for r in w0_rows]
            for jg in range(N_KEYS // BF16_ROWS):
                halves = []
                for hf in range(2):
                    j0 = jg * BF16_ROWS + hf * SUBLANES
                    js = slice(j0, j0 + SUBLANES)
                    g = None
                    for h in range(PEER_HEADS):
                        w1 = w1_ref[h, js, ls]
                        t = jnp.where(r1_ref[h, js, ls] < cnt_b[h], w1, jnp.zeros_like(w1)) * w0_b[h]
                        g = t if g is None else g + t
                    halves.append(g)
                r0 = ii * N_KEYS + jg * BF16_ROWS
                rows = slice(r0, r0 + BF16_ROWS)
                c_cur[rows, ls] = jnp.concatenate(halves, axis=0) * act_cur[rows, ls]

    cur = lax.rem(s + 1, 2)
    prev = lax.rem(s, 2)
    acc_s[...] += jnp.dot(vt_ref[...], c_s[prev], preferred_element_type=F32)
    for ii in range(PEER_IBLK):
        gate_rows(ii, act_s.at[cur], c_s.at[cur])
    act_s[prev] = _gelu(jnp.dot(u_ref[...], ht_ref[...], preferred_element_type=F32)).astype(BF16)

    @pl.when(s == nblk + 1)
    def _():
        y = x_ref[...] + acc_s[...].T
        y_ref[...] = _rms(y, gf_ref[...])


def _peer_dense(x, ht, r1, w1, cnt, w0, u, vt, g_final, tile):
    n_tok, d = x.shape
    n_exp = u.shape[0]
    nblk = n_exp // PEER_EBLK
    hblock = pl.BlockSpec((PEER_HEADS, N_KEYS, tile), lambda t, s: (0, 0, t))
    return pl.pallas_call(
        _peer_dense_kernel, grid=(n_tok // tile, nblk + 2),
        in_specs=[pl.BlockSpec((tile, d), lambda t, s: (t, 0)),
                  pl.BlockSpec((d, tile), lambda t, s: (0, t)),
                  hblock, hblock, hblock, hblock,
                  pl.BlockSpec((PEER_EBLK, d), lambda t, s: (jnp.minimum(s, nblk - 1), 0)),
                  pl.BlockSpec((d, PEER_EBLK), lambda t, s: (0, jnp.clip(s - 2, 0, nblk - 1))),
                  pl.BlockSpec((1, d), lambda t, s: (0, 0))],
        out_specs=pl.BlockSpec((tile, d), lambda t, s: (t, 0)),
        out_shape=jax.ShapeDtypeStruct((n_tok, d), F32),
        scratch_shapes=[pltpu.VMEM((d, tile), F32), pltpu.VMEM((2, PEER_EBLK, tile), BF16),
                        pltpu.VMEM((2, PEER_EBLK, tile), BF16)],
        compiler_params=_cparams("parallel", "arbitrary"), name="peer_dense")(
            x, ht, r1, w1, cnt, w0, u, vt, g_final.reshape(1, d))


def _pad_rows(a, rows):
    return jnp.pad(a, ((0, 0), (rows - a.shape[1], 0), (0, 0)))


def _layer(x3, mem_k, mem_v, a_prev, b_prev, wts, tile):
    bsz, t, d = x3.shape
    n_tok = bsz * t
    x = x3.reshape(n_tok, d)
    za, zb, zif, zg = _norm_proj(x, wts["g_mix"], [wts["w_a"], wts["w_b"], wts["w_if"], wts["w_g"]], tile)

    if a_prev is None:
        band = (BAND_CHUNKS + 1) * CHUNK
        q_pos = np.arange(CHUNK) + BAND_CHUNKS * CHUNK
        bias = _pair_bias(wts["rel_bias"], q_pos, np.arange(band))
        out_a = _band_prompt(za, bias, bsz, t)
        L = CHUNK
        conv_prev = jnp.zeros((bsz, SUBLANES, 2 * B_WIDTH), F32)
        c0 = jnp.zeros((bsz, B_HEADS, B_HEAD_DIM, B_HEAD_DIM), F32)
        n0 = jnp.zeros((bsz, SUBLANES, B_HEAD_DIM), F32)
        m0 = jnp.zeros((bsz, SUBLANES, LANES), F32)
    else:
        k_prev, v_prev = a_prev
        past = k_prev.shape[1]
        za3 = za.reshape(bsz, t, 3 * A_WIDTH)
        k_all = jnp.concatenate([k_prev.reshape(bsz, past, A_WIDTH), za3[:, :, A_WIDTH:2 * A_WIDTH]], axis=1)
        v_all = jnp.concatenate([v_prev.reshape(bsz, past, A_WIDTH), za3[:, :, 2 * A_WIDTH:]], axis=1)
        bias = _pair_bias(wts["rel_bias"], past + np.arange(t), np.arange(past + t))
        out_a = _band_cached(za, k_all, v_all, bias, bsz, t)
        L = t
        conv_prev, c0, n0, m0 = b_prev
        conv_prev = _pad_rows(conv_prev, SUBLANES)
        n0 = jnp.pad(n0, ((0, 0), (0, SUBLANES - B_HEADS), (0, 0)))
        m0 = jnp.broadcast_to(jnp.pad(m0, ((0, 0), (0, SUBLANES - B_HEADS)))[:, :, None], (bsz, SUBLANES, LANES))

    hb, c_new, n_new, m_new = _mlstm(zb, zif, wts["conv_w"], wts["conv_b"], wts["b_if"], wts["g_head"],
                                     conv_prev, c0, n0, m0, bsz, t, L)

    x1 = _merge(x, out_a, hb, zg, wts["w_a_up"], wts["w_b_up"], wts["w_out"], tile)
    x2 = _cross(x1, wts["g_cross"], mem_k, mem_v, wts["w_cq"], wts["w_co"], min(tile, t), t // min(tile, t))

    ht, r1, w1, cnt, w0 = _peer_route(x2, wts["g_ffn"], wts["w_pq_t"], wts["sub_keys"], PEER_TOK)
    y = _peer_dense(x2, ht, r1, w1, cnt, w0, wts["peer_u"], wts["peer_v_t"], wts["g_final"], PEER_TOK)

    za3 = za.reshape(bsz, t, 3 * A_WIDTH)
    keep = min(BAND_CHUNKS * CHUNK, t) if a_prev is None else t
    new_ak = za3[:, t - keep:, A_WIDTH:2 * A_WIDTH].reshape(bsz, keep, A_HEADS, A_HEAD_DIM)
    new_av = za3[:, t - keep:, 2 * A_WIDTH:].reshape(bsz, keep, A_HEADS, A_HEAD_DIM)
    new_conv = zb.reshape(bsz, t, -1)[:, t - (CONV_W - 1):, :2 * B_WIDTH]
    if t < CONV_W - 1:
        raise NotImplementedError("fewer new frames than the conv history")
    return (y.reshape(bsz, t, d),
            (new_ak, new_av, new_conv, c_new, n_new[:, :B_HEADS, :], m_new[:, :B_HEADS, 0]))


def kernel(x_prompt, x_sample, mem_prompt, cache_a_k, cache_a_v, state_b_conv, state_b_C, state_b_n, state_b_m, cache_mem_k, cache_mem_v, g_mix, w_in, conv_w, conv_b, b_if, g_head, rel_bias, w_a_up, w_b_up, w_out, g_mem, w_mk, w_mv, g_cross, w_cq, w_co, g_ffn, w_pq, sub_keys, peer_u, peer_v, g_final):
    depth = w_in.shape[0]
    assert depth == 1, "the final norm is fused into the last PEER call; one layer supported"
    l = 0
    bsz_p, t_p, d = x_prompt.shape
    bsz_s, t_s, _ = x_sample.shape

    wi = w_in[l]
    o_b = 3 * A_WIDTH
    o_if = o_b + 4 * B_WIDTH
    o_g = o_if + 2 * B_HEADS
    wts = {
        "g_mix": g_mix[l],
        "w_a": wi[:, :o_b].astype(BF16),
        "w_b": wi[:, o_b:o_if].astype(BF16),
        "w_if": jnp.pad(wi[:, o_if:o_g], ((0, 0), (0, IF_PAD - 2 * B_HEADS))).astype(BF16),
        "w_g": wi[:, o_g:].astype(BF16),
        "conv_w": conv_w[l], "conv_b": conv_b[l].reshape(1, -1),
        "b_if": jnp.pad(b_if[l], (0, IF_PAD - 2 * B_HEADS)).reshape(1, IF_PAD),
        "g_head": g_head[l].reshape(1, -1),
        "rel_bias": rel_bias[l],
        "w_a_up": w_a_up[l].astype(BF16), "w_b_up": w_b_up[l].astype(BF16), "w_out": w_out[l].astype(BF16),
        "g_cross": g_cross[l], "w_cq": w_cq[l].astype(BF16), "w_co": w_co[l].astype(BF16),
        "g_ffn": g_ffn[l],
        "w_pq_t": w_pq[l].T.astype(BF16),
        "sub_keys": sub_keys[l].reshape(2 * PEER_HEADS, N_KEYS, PEER_HALF).astype(BF16),
        "peer_u": peer_u[l].astype(BF16),
        "peer_v_t": peer_v[l].T.astype(BF16),
        "g_final": g_final,
    }

    mk_flat, mv_flat = _norm_proj(mem_prompt.reshape(bsz_p * MEM_TOKENS, d), g_mem[l],
                                  [w_mk[l].astype(BF16), w_mv[l].astype(BF16)], 256)
    mk_p = mk_flat.reshape(bsz_p, MEM_TOKENS, d)
    mv_p = mv_flat.reshape(bsz_p, MEM_TOKENS, d)

    yp, sp = _layer(x_prompt, mk_p, mv_p, None, None, wts, 256)
    ys, sq = _layer(x_sample, cache_mem_k[l].reshape(bsz_s, MEM_TOKENS, d), cache_mem_v[l].reshape(bsz_s, MEM_TOKENS, d),
                    (cache_a_k[l], cache_a_v[l]), (state_b_conv[l], state_b_C[l], state_b_n[l], state_b_m[l]), wts, 256)

    ps = sp + (mk_p.reshape(bsz_p, MEM_TOKENS, X_HEADS, X_HEAD_DIM), mv_p.reshape(bsz_p, MEM_TOKENS, X_HEADS, X_HEAD_DIM))
    return (yp, ys) + tuple(a[None] for a in ps) + tuple(a[None] for a in sq)
```

```python
import numpy as np
import jax
import jax.numpy as jnp
from jax import lax
from jax.experimental import pallas as pl
from jax.experimental.pallas import tpu as pltpu

F32 = jnp.float32
BF16 = jnp.bfloat16

D_MODEL = 1024
CHUNK = 64
A_HEADS = 8
A_HEAD_DIM = 64
A_WIDTH = A_HEADS * A_HEAD_DIM
BAND_CHUNKS = 8
MAX_REL = 128
B_HEADS = 4
B_HEAD_DIM = 128
B_WIDTH = B_HEADS * B_HEAD_DIM
CONV_W = 4
MEM_TOKENS = 256
X_HEADS = 4
X_HEAD_DIM = D_MODEL // X_HEADS
PEER_HEADS = 8
N_KEYS = 128
PEER_TOPK = 16
PEER_QDIM = 256
PEER_HALF = PEER_QDIM // 2
EPS = 1e-6

LANES = 128
SUBLANES = 8
VMEM_LIMIT = 56 * 1024 * 1024
NEG_BIG = -1e30

A_PAIRS = A_HEADS // 2
IF_PAD = LANES
PEER_TOK = 512
PEER_EBLK = 512
PEER_IBLK = PEER_EBLK // N_KEYS


def _cparams(*sem):
    return pltpu.CompilerParams(dimension_semantics=sem, vmem_limit_bytes=VMEM_LIMIT)


def _rms(xf, g):
    r = lax.rsqrt(jnp.mean(xf * xf, axis=-1, keepdims=True) + EPS)
    return xf * r * g


def _sigmoid(x):
    return 1.0 / (1.0 + jnp.exp(-x))


def _log_sigmoid(x):
    return jnp.minimum(x, 0.0) - jnp.log1p(jnp.exp(-jnp.abs(x)))


def _norm_proj_kernel(x_ref, g_ref, *refs):
    n = len(refs) // 2
    h = _rms(x_ref[...], g_ref[...]).astype(BF16)
    for w_ref, o_ref in zip(refs[:n], refs[n:]):
        o_ref[...] = jnp.dot(h, w_ref[...], preferred_element_type=F32)


def _norm_proj(x, g, ws, tile):
    n_tok, d = x.shape
    assert n_tok % tile == 0
    in_specs = [pl.BlockSpec((tile, d), lambda i: (i, 0)), pl.BlockSpec((1, d), lambda i: (0, 0))]
    in_specs += [pl.BlockSpec(w.shape, lambda i: (0, 0)) for w in ws]
    out_specs = [pl.BlockSpec((tile, w.shape[1]), lambda i: (i, 0)) for w in ws]
    out_shape = [jax.ShapeDtypeStruct((n_tok, w.shape[1]), F32) for w in ws]
    return pl.pallas_call(
        _norm_proj_kernel, grid=(n_tok // tile,), in_specs=in_specs, out_specs=out_specs, out_shape=out_shape,
        compiler_params=_cparams("parallel"), name="norm_proj")(x, g.reshape(1, d), *ws)


def _attn_scores(qp, kb, bias, valid_from):
    lane = lax.broadcasted_iota(jnp.int32, qp.shape, 1)
    q0 = jnp.where(lane < A_HEAD_DIM, qp, 0.0)
    q1 = jnp.where(lane >= A_HEAD_DIM, qp, 0.0)
    qs = jnp.concatenate([q0, q1], axis=0).astype(BF16)
    s = lax.dot_general(qs, kb, (((1,), (1,)), ((), ())), preferred_element_type=F32) + bias
    if valid_from is not None:
        col = lax.broadcasted_iota(jnp.int32, s.shape, 1)
        s = jnp.where(col >= valid_from, s, NEG_BIG)
    return s


def _attn_probs(s):
    mx = jnp.max(s, axis=-1, keepdims=True)
    p = jnp.exp(s - mx)
    return p, jnp.sum(p, axis=-1, keepdims=True)


def _attn_out(p, den, vb):
    m_rows = p.shape[0] // 2
    o = jnp.dot(p.astype(BF16), vb, preferred_element_type=F32) / den
    lane = lax.broadcasted_iota(jnp.int32, (m_rows, LANES), 1)
    return jnp.where(lane < A_HEAD_DIM, o[:m_rows], o[m_rows:])


def _attn_pair(qp, kb, vb, bias, valid_from):
    p, den = _attn_probs(_attn_scores(qp, kb, bias, valid_from))
    return _attn_out(p, den, vb)


def _band_prompt_kernel(q_ref, k_ref, v_ref, bias_ref, o_ref, kpad, vpad):
    c = pl.program_id(1)
    pad = BAND_CHUNKS * CHUNK
    width = pad + CHUNK

    @pl.when(c == 0)
    def _():
        kpad[0:pad, :] = jnp.zeros((pad, A_WIDTH), BF16)
        vpad[0:pad, :] = jnp.zeros((pad, A_WIDTH), BF16)
        kpad[pad:, :] = k_ref[...].astype(BF16)
        vpad[pad:, :] = v_ref[...].astype(BF16)

    per = q_ref.shape[0] // CHUNK
    units = [(k, hp) for k in range(per) for hp in range(A_PAIRS)]

    def where(u):
        k, hp = u
        start = pl.multiple_of((c * per + k) * CHUNK, CHUNK)
        return slice(k * CHUNK, (k + 1) * CHUNK), slice(hp * LANES, (hp + 1) * LANES), start

    scores = {}
    for u in units:
        rows, cols, start = where(u)
        qp = q_ref[rows, cols] * (A_HEAD_DIM ** -0.5)
        kb = kpad[pl.ds(start, width), cols]
        scores[u] = _attn_scores(qp, kb, bias_ref[u[1]], pad - (c * per + u[0]) * CHUNK)
    probs = {u: _attn_probs(scores[u]) for u in units}
    for u in units:
        rows, cols, start = where(u)
        vb = vpad[pl.ds(start, width), cols]
        o_ref[rows, cols] = _attn_out(*probs[u], vb)


BAND_CHUNKS_PER_STEP = 4


def _band_prompt(za, bias, bsz, t):
    per = BAND_CHUNKS_PER_STEP if (t // CHUNK) % BAND_CHUNKS_PER_STEP == 0 else 1
    nc = t // (CHUNK * per)
    pad = BAND_CHUNKS * CHUNK
    return pl.pallas_call(
        _band_prompt_kernel, grid=(bsz, nc),
        in_specs=[pl.BlockSpec((CHUNK * per, A_WIDTH), lambda b, c: (b * nc + c, 0)),
                  pl.BlockSpec((t, A_WIDTH), lambda b, c: (b, 1)),
                  pl.BlockSpec((t, A_WIDTH), lambda b, c: (b, 2)),
                  pl.BlockSpec(bias.shape, lambda b, c: (0, 0, 0))],
        out_specs=pl.BlockSpec((CHUNK * per, A_WIDTH), lambda b, c: (b * nc + c, 0)),
        out_shape=jax.ShapeDtypeStruct((bsz * t, A_WIDTH), F32),
        scratch_shapes=[pltpu.VMEM((pad + t, A_WIDTH), BF16), pltpu.VMEM((pad + t, A_WIDTH), BF16)],
        compiler_params=_cparams("parallel", "arbitrary"), name="band_prompt")(za, za, za, bias)


def _band_cached_kernel(q_ref, k_ref, v_ref, bias_ref, o_ref):
    for hp in range(A_PAIRS):
        cols = slice(hp * LANES, (hp + 1) * LANES)
        qp = q_ref[:, cols] * (A_HEAD_DIM ** -0.5)
        kb = k_ref[:, cols].astype(BF16)
        vb = v_ref[:, cols].astype(BF16)
        o_ref[:, cols] = _attn_pair(qp, kb, vb, bias_ref[hp], None)


def _band_cached(za, k_all, v_all, bias, bsz, s_len):
    lk = k_all.shape[1]
    return pl.pallas_call(
        _band_cached_kernel, grid=(bsz,),
        in_specs=[pl.BlockSpec((s_len, A_WIDTH), lambda b: (b, 0)),
                  pl.BlockSpec((None, lk, A_WIDTH), lambda b: (b, 0, 0)),
                  pl.BlockSpec((None, lk, A_WIDTH), lambda b: (b, 0, 0)),
                  pl.BlockSpec(bias.shape, lambda b: (0, 0, 0))],
        out_specs=pl.BlockSpec((s_len, A_WIDTH), lambda b: (b, 0)),
        out_shape=jax.ShapeDtypeStruct((bsz * s_len, A_WIDTH), F32),
        compiler_params=_cparams("parallel"), name="band_cached")(za, k_all, v_all, bias)


def _pair_bias(rel_bias, q_pos, k_pos):
    nq, nk = q_pos.shape[0], k_pos.shape[0]
    assert np.all(np.diff(q_pos) == 1) and np.all(np.diff(k_pos) == 1), "Toeplitz layout needs consecutive positions"
    d_max = int(q_pos[-1] - k_pos[0])
    idx = np.clip(d_max - np.arange(nq + nk - 1), -MAX_REL, MAX_REL) + MAX_REL
    u = jnp.take(rel_bias, jnp.asarray(idx, jnp.int32), axis=1)
    full = jnp.stack([lax.slice_in_dim(u, nq - 1 - q, nq - 1 - q + nk, axis=1) for q in range(nq)], axis=1)
    return full.reshape(A_PAIRS, 2 * nq, nk).astype(F32)


def _mlstm_kernel(zb_all, zif_all, convw_ref, convb_ref, bif_ref, ghead_ref, cprev_ref, c0_ref, n0_ref, m0_ref,
                  hb_all, c_out, n_out, m_out, c_all, n_all, m_all, ubuf_all):
    c = pl.program_id(1)
    nchunks = pl.num_programs(1)
    nseq, L = zif_all.shape[0], zif_all.shape[1]
    hist = SUBLANES

    @pl.when(c == 0)
    def _():
        c_all[...] = c0_ref[...]
        n_all[...] = n0_ref[...]
        m_all[...] = m0_ref[...]
        ubuf_all[:, 0:hist, :] = cprev_ref[...]

    _mlstm_group(zb_all, zif_all, convw_ref, convb_ref, bif_ref, ghead_ref, hb_all, c_all, n_all, m_all, ubuf_all, L)

    @pl.when(c == nchunks - 1)
    def _():
        c_out[...] = c_all[...]
        n_out[...] = n_all[...]
        m_out[...] = m_all[...]


def _mlstm_group(zb_all, zif_all, convw_ref, convb_ref, bif_ref, ghead_ref, hb_all, c_all, n_all, m_all, ubuf_all, L):
    nseq = zif_all.shape[0]
    qk_w = 2 * B_WIDTH
    hist = SUBLANES
    hi = lax.Precision.HIGHEST
    eye = (lax.broadcasted_iota(jnp.int32, (LANES, LANES), 0) ==
           lax.broadcasted_iota(jnp.int32, (LANES, LANES), 1)).astype(F32)
    row = lax.broadcasted_iota(jnp.int32, (L, L), 0)
    colm = lax.broadcasted_iota(jnp.int32, (L, L), 1)
    causal = colm <= row
    lower = causal.astype(F32)
    upper = (row <= colm).astype(F32)

    qk, gates = [], []
    for bb in range(nseq):
        ubuf = ubuf_all.at[bb]
        ubuf[hist:hist + L, :] = zb_all[bb, :, 0:qk_w]
        acc = jnp.broadcast_to(convb_ref[...], (L, qk_w))
        for j in range(CONV_W):
            off = hist - (CONV_W - 1) + j
            acc = acc + ubuf[off:off + L, :] * convw_ref[j:j + 1, :]
        ubuf[0:hist, :] = ubuf[L:L + hist, :]
        qk.append(acc * _sigmoid(acc))
        gates.append(zif_all[bb] + bif_ref[...])
    gates_t = [lax.dot_general(eye, g, (((1,), (1,)), ((), ())), precision=hi, preferred_element_type=F32)
               for g in gates]
    b_col_all = [jnp.dot(lower, _log_sigmoid(g), precision=hi, preferred_element_type=F32) for g in gates]
    b_row_all = [jnp.dot(_log_sigmoid(gt), upper, precision=hi, preferred_element_type=F32) for gt in gates_t]

    units = [(bb, h) for bb in range(nseq) for h in range(B_HEADS)]
    st = {}
    for u in units:
        bb, h = u
        d = {}
        d["qh"] = qk[bb][:, h * B_HEAD_DIM:(h + 1) * B_HEAD_DIM]
        d["kh"] = qk[bb][:, B_WIDTH + h * B_HEAD_DIM:B_WIDTH + (h + 1) * B_HEAD_DIM] * (B_HEAD_DIM ** -0.5)
        vh = zb_all[bb, :, qk_w + h * B_HEAD_DIM:qk_w + (h + 1) * B_HEAD_DIM]
        d["qb"], d["kb"], d["vb"] = d["qh"].astype(BF16), d["kh"].astype(BF16), vh.astype(BF16)
        d["ig_col"] = gates[bb][:, h:h + 1]
        ig_row = gates_t[bb][h:h + 1, :]
        d["b_col"] = b_col_all[bb][:, B_HEADS + h:B_HEADS + h + 1]
        b_row = b_row_all[bb][B_HEADS + h:B_HEADS + h + 1, :]
        d["m0"] = m_all[bb, h:h + 1, 0:1]
        d["c0"] = c_all[bb, h]
        d["n0"] = n_all[bb, h:h + 1, :]
        dm = jnp.where(causal, d["b_col"] - b_row + ig_row, NEG_BIG)
        inter = d["b_col"] + d["m0"]
        d["m"] = jnp.maximum(jnp.max(dm, axis=-1, keepdims=True), inter)
        d["w_intra"] = jnp.exp(dm - d["m"])
        d["w_inter"] = jnp.exp(inter - d["m"])
        st[u] = d
    for u in units:
        d = st[u]
        d["a"] = lax.dot_general(d["qb"], d["kb"], (((1,), (1,)), ((), ())),
                                 preferred_element_type=F32) * d["w_intra"]
    for u in units:
        d = st[u]
        num = (jnp.dot(d["a"].astype(BF16), d["vb"], preferred_element_type=F32) +
               d["w_inter"] * jnp.dot(d["qb"], d["c0"].astype(BF16), preferred_element_type=F32))
        den = (jnp.sum(d["a"], axis=-1, keepdims=True) +
               d["w_inter"] * jnp.sum(d["qh"] * d["n0"], axis=-1, keepdims=True))
        d["hh"] = num / jnp.maximum(jnp.abs(den), jnp.exp(-d["m"]))
    for u in units:
        bb, h = u
        d = st[u]
        bl = d["b_col"][L - 1:L, :]
        wk = bl - d["b_col"] + d["ig_col"]
        ml = jnp.maximum(bl + d["m0"], jnp.max(wk, axis=0, keepdims=True))
        a0 = jnp.exp(bl + d["m0"] - ml)
        ws = jnp.exp(wk - ml)
        kw = d["kh"] * ws
        c_all[bb, h] = a0 * d["c0"] + lax.dot_general(kw.astype(BF16), d["vb"], (((0,), (0,)), ((), ())),
                                                      preferred_element_type=F32)
        n_all[bb, h:h + 1, :] = a0 * d["n0"] + jnp.sum(kw, axis=0, keepdims=True)
        m_all[bb, h:h + 1, :] = jnp.broadcast_to(ml, (1, LANES))
    for u in units:
        bb, h = u
        cols = slice(h * B_HEAD_DIM, (h + 1) * B_HEAD_DIM)
        oh = zb_all[bb, :, qk_w + B_WIDTH + h * B_HEAD_DIM:qk_w + B_WIDTH + (h + 1) * B_HEAD_DIM]
        hn = _rms(st[u]["hh"], ghead_ref[:, cols])
        hb_all[bb, :, cols] = _sigmoid(oh) * hn


MLSTM_SEQ_PER_STEP = 4


def _mlstm(zb, zif, conv_w, conv_b, bif, g_head, conv_prev, c0, n0, m0, bsz, t, L):
    nc = t // L
    qk_w = 2 * B_WIDTH
    hd = B_HEAD_DIM
    g = next(k for k in (MLSTM_SEQ_PER_STEP, 2, 1) if bsz % k == 0)
    full2 = lambda b, c: (0, 0)
    seq3 = lambda b, c: (b, c, 0)
    st3 = lambda b, c: (b, 0, 0)
    outs = pl.pallas_call(
        _mlstm_kernel, grid=(bsz // g, nc),
        in_specs=[pl.BlockSpec((g, L, zb.shape[1]), seq3),
                  pl.BlockSpec((g, L, IF_PAD), seq3),
                  pl.BlockSpec((CONV_W, qk_w), full2),
                  pl.BlockSpec((1, qk_w), full2),
                  pl.BlockSpec((1, IF_PAD), full2),
                  pl.BlockSpec((1, B_WIDTH), full2),
                  pl.BlockSpec((g, SUBLANES, qk_w), st3),
                  pl.BlockSpec((g, B_HEADS, hd, hd), lambda b, c: (b, 0, 0, 0)),
                  pl.BlockSpec((g, SUBLANES, hd), st3),
                  pl.BlockSpec((g, SUBLANES, LANES), st3)],
        out_specs=[pl.BlockSpec((g, L, B_WIDTH), seq3),
                   pl.BlockSpec((g, B_HEADS, hd, hd), lambda b, c: (b, 0, 0, 0)),
                   pl.BlockSpec((g, SUBLANES, hd), st3),
                   pl.BlockSpec((g, SUBLANES, LANES), st3)],
        out_shape=[jax.ShapeDtypeStruct((bsz, t, B_WIDTH), F32),
                   jax.ShapeDtypeStruct((bsz, B_HEADS, hd, hd), F32),
                   jax.ShapeDtypeStruct((bsz, SUBLANES, hd), F32),
                   jax.ShapeDtypeStruct((bsz, SUBLANES, LANES), F32)],
        scratch_shapes=[pltpu.VMEM((g, B_HEADS, hd, hd), F32), pltpu.VMEM((g, SUBLANES, hd), F32),
                        pltpu.VMEM((g, SUBLANES, LANES), F32), pltpu.VMEM((g, SUBLANES + L, qk_w), F32)],
        compiler_params=_cparams("parallel", "arbitrary"), name="mlstm")(
            zb.reshape(bsz, t, -1), zif.reshape(bsz, t, -1), conv_w, conv_b, bif, g_head, conv_prev, c0, n0, m0)
    return (outs[0].reshape(bsz * t, B_WIDTH),) + tuple(outs[1:])


def _merge_kernel(x_ref, oa_ref, hb_ref, zg_ref, wa_ref, wb_ref, wo_ref, o_ref):
    ua = jnp.dot(oa_ref[...].astype(BF16), wa_ref[...], preferred_element_type=F32)
    ub = jnp.dot(hb_ref[...].astype(BF16), wb_ref[...], preferred_element_type=F32)
    mixed = _sigmoid(zg_ref[:, 0:D_MODEL]) * ua + _sigmoid(zg_ref[:, D_MODEL:]) * ub
    o_ref[...] = x_ref[...] + jnp.dot(mixed.astype(BF16), wo_ref[...], preferred_element_type=F32)


def _merge(x, oa, hb, zg, wa, wb, wo, tile):
    n_tok, d = x.shape
    row = lambda i: (i, 0)
    full = lambda i: (0, 0)
    return pl.pallas_call(
        _merge_kernel, grid=(n_tok // tile,),
        in_specs=[pl.BlockSpec((tile, d), row), pl.BlockSpec((tile, A_WIDTH), row), pl.BlockSpec((tile, B_WIDTH), row),
                  pl.BlockSpec((tile, 2 * d), row), pl.BlockSpec(wa.shape, full), pl.BlockSpec(wb.shape, full),
                  pl.BlockSpec(wo.shape, full)],
        out_specs=pl.BlockSpec((tile, d), row), out_shape=jax.ShapeDtypeStruct((n_tok, d), F32),
        compiler_params=_cparams("parallel"), name="merge")(x, oa, hb, zg, wa, wb, wo)


def _cross_kernel(x_ref, g_ref, k_ref, v_ref, wq_ref, wo_ref, o_ref):
    x = x_ref[...]
    hn = _rms(x, g_ref[...]).astype(BF16)
    q = jnp.dot(hn, wq_ref[...], preferred_element_type=F32) * (X_HEAD_DIM ** -0.5)
    outs = []
    for h in range(X_HEADS):
        cols = slice(h * X_HEAD_DIM, (h + 1) * X_HEAD_DIM)
        qh = q[:, cols].astype(BF16)
        kh = k_ref[:, cols].astype(BF16)
        vh = v_ref[:, cols].astype(BF16)
        s = lax.dot_general(qh, kh, (((1,), (1,)), ((), ())), preferred_element_type=F32)
        mx = jnp.max(s, axis=-1, keepdims=True)
        p = jnp.exp(s - mx)
        den = jnp.sum(p, axis=-1, keepdims=True)
        outs.append((jnp.dot(p.astype(BF16), vh, preferred_element_type=F32) / den).astype(BF16))
    o = jnp.concatenate(outs, axis=-1)
    o_ref[...] = x + jnp.dot(o, wo_ref[...], preferred_element_type=F32)


def _cross(x, g, mem_k, mem_v, wq, wo, tile, tiles_per_batch):
    n_tok, d = x.shape
    row = lambda i: (i, 0)
    full = lambda i: (0, 0)
    mem = lambda i: (i // tiles_per_batch, 0, 0)
    return pl.pallas_call(
        _cross_kernel, grid=(n_tok // tile,),
        in_specs=[pl.BlockSpec((tile, d), row), pl.BlockSpec((1, d), full),
                  pl.BlockSpec((None, MEM_TOKENS, d), mem), pl.BlockSpec((None, MEM_TOKENS, d), mem),
                  pl.BlockSpec(wq.shape, full), pl.BlockSpec(wo.shape, full)],
        out_specs=pl.BlockSpec((tile, d), row), out_shape=jax.ShapeDtypeStruct((n_tok, d), F32),
        compiler_params=_cparams("parallel"), name="cross")(x, g.reshape(1, d), mem_k, mem_v, wq, wo)


def _oddeven_mergesort_pairs(n):
    pairs = []
    p = 1
    while p < n:
        k = p
        while k >= 1:
            for j in range(k % p, n - k, 2 * k):
                for i in range(min(k, n - j - k)):
                    if (i + j) // (2 * p) == (i + j + k) // (2 * p):
                        pairs.append((i + j, i + j + k))
            k //= 2
        p *= 2
    return pairs


_SORT16 = _oddeven_mergesort_pairs(PEER_TOPK)
PEER_ROWCHUNK = 32


def _peer_route_kernel(x_ref, g_ref, wpq_ref, sk_ref, ht_ref, r1_ref, w1_ref, cnt_ref, w0_ref, qt_s, s_s, sv_s, th_s):
    hn = _rms(x_ref[...], g_ref[...])
    ht = hn.T.astype(BF16)
    ht_ref[...] = ht
    qt_s[...] = jnp.dot(wpq_ref[...], ht, preferred_element_type=F32)
    ngrp = N_KEYS // SUBLANES
    assert ngrp == PEER_TOPK

    def score_one(hc):
        off = pl.multiple_of(hc * PEER_HALF, PEER_HALF)
        qt = qt_s[pl.ds(off, PEER_HALF), :].astype(BF16)
        st = jnp.dot(sk_ref[hc], qt, preferred_element_type=F32)
        s_s[hc] = st
        w = [st[SUBLANES * k:SUBLANES * (k + 1), :] for k in range(ngrp)]
        for a, b in _SORT16:
            w[a], w[b] = jnp.maximum(w[a], w[b]), jnp.minimum(w[a], w[b])
        tops = []
        for r in range(PEER_TOPK):
            mx = jnp.max(w[0], axis=0, keepdims=True)
            tops.append(mx)
            hit = w[0] == mx
            for k in range(PEER_TOPK - 1 - r):
                w[k] = jnp.where(hit, w[k + 1], w[k])
        sv_s[hc] = jnp.concatenate(tops, axis=0)

    def score_body(h, carry):
        score_one(2 * h)
        score_one(2 * h + 1)
        return carry

    lax.fori_loop(0, PEER_HEADS, score_body, 0)

    def head_body(h, carry):
        sv0 = sv_s[2 * h]
        sv1 = sv_s[2 * h + 1]
        half = PEER_TOPK // 2
        pieces = [sv0[0:1] + sv1, sv0[1:2] + sv1[0:half]]
        pieces += [sv0[a:a + 1] + sv1[0:half] for a in range(2, half)]
        pieces += [sv0[half:] + sv1[0:1]]
        cand = jnp.concatenate(pieces, axis=0)
        top = sv0[0:1] + sv1[0:1]
        z = jnp.zeros_like(top)
        tau = top
        for _ in range(PEER_TOPK):
            tau = jnp.max(cand, axis=0, keepdims=True)
            z = z + jnp.exp(tau - top)
            cand = jnp.where(cand == tau, NEG_BIG, cand)
        for b in range(PEER_TOPK):
            sel = (sv0 + sv1[b:b + 1]) >= tau
            th_s[b:b + 1, :] = jnp.min(jnp.where(sel, sv0, -NEG_BIG), axis=0, keepdims=True)
        inv_z = 1.0 / z
        for kk in range(0, N_KEYS, PEER_ROWCHUNK):
            rows = slice(kk, kk + PEER_ROWCHUNK)
            s0 = s_s[2 * h, rows, :]
            s1 = s_s[2 * h + 1, rows, :]
            cnt = jnp.zeros_like(s0)
            rank = jnp.zeros_like(s1)
            for b in range(PEER_TOPK):
                cnt = jnp.where(s0 >= th_s[b:b + 1, :], float(b + 1), cnt)
                rank = jnp.where(sv1[b:b + 1] > s1, float(b + 1), rank)
            cnt_ref[h, rows, :] = cnt
            r1_ref[h, rows, :] = rank.astype(BF16)
            w0_ref[h, rows, :] = jnp.exp(s0 - sv0[0:1])
            w1_ref[h, rows, :] = (jnp.exp(s1 - sv1[0:1]) * inv_z).astype(BF16)
        return carry

    lax.fori_loop(0, PEER_HEADS, head_body, 0)


def _peer_route(x, g, wpq_t, sk, tile):
    n_tok, d = x.shape
    nhc = 2 * PEER_HEADS
    col3 = lambda i: (0, 0, i)
    hshape = (PEER_HEADS, N_KEYS, n_tok)
    hblock = pl.BlockSpec((PEER_HEADS, N_KEYS, tile), col3)
    return pl.pallas_call(
        _peer_route_kernel, grid=(n_tok // tile,),
        in_specs=[pl.BlockSpec((tile, d), lambda i: (i, 0)), pl.BlockSpec((1, d), lambda i: (0, 0)),
                  pl.BlockSpec(wpq_t.shape, lambda i: (0, 0)), pl.BlockSpec(sk.shape, lambda i: (0, 0, 0))],
        out_specs=[pl.BlockSpec((d, tile), lambda i: (0, i)), hblock, hblock, hblock, hblock],
        out_shape=[jax.ShapeDtypeStruct((d, n_tok), BF16),
                   jax.ShapeDtypeStruct(hshape, BF16), jax.ShapeDtypeStruct(hshape, BF16),
                   jax.ShapeDtypeStruct(hshape, F32), jax.ShapeDtypeStruct(hshape, F32)],
        scratch_shapes=[pltpu.VMEM((PEER_HEADS * PEER_QDIM, tile), F32), pltpu.VMEM((nhc, N_KEYS, tile), F32),
                        pltpu.VMEM((nhc, PEER_TOPK, tile), F32), pltpu.VMEM((PEER_TOPK, tile), F32)],
        compiler_params=_cparams("parallel"), name="peer_route")(x, g.reshape(1, d), wpq_t, sk)


def _gelu(z):
    return 0.5 * z * (1.0 + lax.erf(z * (2.0 ** -0.5)))


BF16_ROWS = 2 * SUBLANES


def _peer_dense_kernel(x_ref, ht_ref, r1_ref, w1_ref, cnt_ref, w0_ref, u_ref, vt_ref, gf_ref, y_ref,
                       acc_s, act_s, c_s):
    s = pl.program_id(1)
    nblk = pl.num_programs(1) - 2
    P = x_ref.shape[0]

    @pl.when(s == 0)
    def _():
        acc_s[...] = jnp.zeros_like(acc_s)
        act_s[...] = jnp.zeros_like(act_s)
        c_s[...] = jnp.zeros_like(c_s)

    blk = jnp.clip(s - 1, 0, nblk - 1)
    slab_l = 2 * LANES

    def gate_rows(ii, act_cur, c_cur):
        i = blk * PEER_IBLK + ii
        cnt_rows = [cnt_ref[h, pl.ds(i, 1), :] for h in range(PEER_HEADS)]
        w0_rows = [w0_ref[h, pl.ds(i, 1), :] for h in range(PEER_HEADS)]
        for lg in range(P // slab_l):
            ls = slice(lg * slab_l, (lg + 1) * slab_l)
            cnt_b = [jnp.broadcast_to(r[:, ls], (SUBLANES, slab_l)).astype(BF16) for r in cnt_rows]
            w0_b = [jnp.broadcast_to(r[:, ls], (SUBLANES, slab_l)).astype(BF16) for r in w0_rows]
            for jg in range(N_KEYS // BF16_ROWS):
                halves = []
                for hf in range(2):
                    j0 = jg * BF16_ROWS + hf * SUBLANES
                    js = slice(j0, j0 + SUBLANES)
                    g = None
                    for h in range(PEER_HEADS):
                        w1 = w1_ref[h, js, ls]
                        t = jnp.where(r1_ref[h, js, ls] < cnt_b[h], w1, jnp.zeros_like(w1)) * w0_b[h]
                        g = t if g is None else g + t
                    halves.append(g)
                r0 = ii * N_KEYS + jg * BF16_ROWS
                rows = slice(r0, r0 + BF16_ROWS)
                c_cur[rows, ls] = jnp.concatenate(halves, axis=0) * act_cur[rows, ls]

    cur = lax.rem(s + 1, 2)
    prev = lax.rem(s, 2)
    acc_s[...] += jnp.dot(vt_ref[...], c_s[prev], preferred_element_type=F32)
    for ii in range(PEER_IBLK):
        gate_rows(ii, act_s.at[cur], c_s.at[cur])
    act_s[prev] = _gelu(jnp.dot(u_ref[...], ht_ref[...], preferred_element_type=F32)).astype(BF16)

    @pl.when(s == nblk + 1)
    def _():
        y = x_ref[...] + acc_s[...].T
        y_ref[...] = _rms(y, gf_ref[...])


def _peer_dense(x, ht, r1, w1, cnt, w0, u, vt, g_final, tile):
    n_tok, d = x.shape
    n_exp = u.shape[0]
    nblk = n_exp // PEER_EBLK
    hblock = pl.BlockSpec((PEER_HEADS, N_KEYS, tile), lambda t, s: (0, 0, t))
    return pl.pallas_call(
        _peer_dense_kernel, grid=(n_tok // tile, nblk + 2),
        in_specs=[pl.BlockSpec((tile, d), lambda t, s: (t, 0)),
                  pl.BlockSpec((d, tile), lambda t, s: (0, t)),
                  hblock, hblock, hblock, hblock,
                  pl.BlockSpec((PEER_EBLK, d), lambda t, s: (jnp.minimum(s, nblk - 1), 0)),
                  pl.BlockSpec((d, PEER_EBLK), lambda t, s: (0, jnp.clip(s - 2, 0, nblk - 1))),
                  pl.BlockSpec((1, d), lambda t, s: (0, 0))],
        out_specs=pl.BlockSpec((tile, d), lambda t, s: (t, 0)),
        out_shape=jax.ShapeDtypeStruct((n_tok, d), F32),
        scratch_shapes=[pltpu.VMEM((d, tile), F32), pltpu.VMEM((2, PEER_EBLK, tile), BF16),
                        pltpu.VMEM((2, PEER_EBLK, tile), BF16)],
        compiler_params=_cparams("parallel", "arbitrary"), name="peer_dense")(
            x, ht, r1, w1, cnt, w0, u, vt, g_final.reshape(1, d))


def _pad_rows(a, rows):
    return jnp.pad(a, ((0, 0), (rows - a.shape[1], 0), (0, 0)))


def _layer(x3, mem_k, mem_v, a_prev, b_prev, wts, tile):
    bsz, t, d = x3.shape
    n_tok = bsz * t
    x = x3.reshape(n_tok, d)
    za, zb, zif, zg = _norm_proj(x, wts["g_mix"], [wts["w_a"], wts["w_b"], wts["w_if"], wts["w_g"]], tile)

    if a_prev is None:
        band = (BAND_CHUNKS + 1) * CHUNK
        q_pos = np.arange(CHUNK) + BAND_CHUNKS * CHUNK
        bias = _pair_bias(wts["rel_bias"], q_pos, np.arange(band))
        out_a = _band_prompt(za, bias, bsz, t)
        L = CHUNK
        conv_prev = jnp.zeros((bsz, SUBLANES, 2 * B_WIDTH), F32)
        c0 = jnp.zeros((bsz, B_HEADS, B_HEAD_DIM, B_HEAD_DIM), F32)
        n0 = jnp.zeros((bsz, SUBLANES, B_HEAD_DIM), F32)
        m0 = jnp.zeros((bsz, SUBLANES, LANES), F32)
    else:
        k_prev, v_prev = a_prev
        past = k_prev.shape[1]
        za3 = za.reshape(bsz, t, 3 * A_WIDTH)
        k_all = jnp.concatenate([k_prev.reshape(bsz, past, A_WIDTH), za3[:, :, A_WIDTH:2 * A_WIDTH]], axis=1)
        v_all = jnp.concatenate([v_prev.reshape(bsz, past, A_WIDTH), za3[:, :, 2 * A_WIDTH:]], axis=1)
        bias = _pair_bias(wts["rel_bias"], past + np.arange(t), np.arange(past + t))
        out_a = _band_cached(za, k_all, v_all, bias, bsz, t)
        L = t
        conv_prev, c0, n0, m0 = b_prev
        conv_prev = _pad_rows(conv_prev, SUBLANES)
        n0 = jnp.pad(n0, ((0, 0), (0, SUBLANES - B_HEADS), (0, 0)))
        m0 = jnp.broadcast_to(jnp.pad(m0, ((0, 0), (0, SUBLANES - B_HEADS)))[:, :, None], (bsz, SUBLANES, LANES))

    hb, c_new, n_new, m_new = _mlstm(zb, zif, wts["conv_w"], wts["conv_b"], wts["b_if"], wts["g_head"],
                                     conv_prev, c0, n0, m0, bsz, t, L)

    x1 = _merge(x, out_a, hb, zg, wts["w_a_up"], wts["w_b_up"], wts["w_out"], tile)
    x2 = _cross(x1, wts["g_cross"], mem_k, mem_v, wts["w_cq"], wts["w_co"], min(tile, t), t // min(tile, t))

    ht, r1, w1, cnt, w0 = _peer_route(x2, wts["g_ffn"], wts["w_pq_t"], wts["sub_keys"], PEER_TOK)
    y = _peer_dense(x2, ht, r1, w1, cnt, w0, wts["peer_u"], wts["peer_v_t"], wts["g_final"], PEER_TOK)

    za3 = za.reshape(bsz, t, 3 * A_WIDTH)
    keep = min(BAND_CHUNKS * CHUNK, t) if a_prev is None else t
    new_ak = za3[:, t - keep:, A_WIDTH:2 * A_WIDTH].reshape(bsz, keep, A_HEADS, A_HEAD_DIM)
    new_av = za3[:, t - keep:, 2 * A_WIDTH:].reshape(bsz, keep, A_HEADS, A_HEAD_DIM)
    new_conv = zb.reshape(bsz, t, -1)[:, t - (CONV_W - 1):, :2 * B_WIDTH]
    if t < CONV_W - 1:
        raise NotImplementedError("fewer new frames than the conv history")
    return (y.reshape(bsz, t, d),
            (new_ak, new_av, new_conv, c_new, n_new[:, :B_HEADS, :], m_new[:, :B_HEADS, 0]))


def kernel(x_prompt, x_sample, mem_prompt, cache_a_k, cache_a_v, state_b_conv, state_b_C, state_b_n, state_b_m, cache_mem_k, cache_mem_v, g_mix, w_in, conv_w, conv_b, b_if, g_head, rel_bias, w_a_up, w_b_up, w_out, g_mem, w_mk, w_mv, g_cross, w_cq, w_co, g_ffn, w_pq, sub_keys, peer_u, peer_v, g_final):
    depth = w_in.shape[0]
    assert depth == 1, "the final norm is fused into the last PEER call; one layer supported"
    l = 0
    bsz_p, t_p, d = x_prompt.shape
    bsz_s, t_s, _ = x_sample.shape

    wi = w_in[l]
    o_b = 3 * A_WIDTH
    o_if = o_b + 4 * B_WIDTH
    o_g = o_if + 2 * B_HEADS
    wts = {
        "g_mix": g_mix[l],
        "w_a": wi[:, :o_b].astype(BF16),
        "w_b": wi[:, o_b:o_if].astype(BF16),
        "w_if": jnp.pad(wi[:, o_if:o_g], ((0, 0), (0, IF_PAD - 2 * B_HEADS))).astype(BF16),
        "w_g": wi[:, o_g:].astype(BF16),
        "conv_w": conv_w[l], "conv_b": conv_b[l].reshape(1, -1),
        "b_if": jnp.pad(b_if[l], (0, IF_PAD - 2 * B_HEADS)).reshape(1, IF_PAD),
        "g_head": g_head[l].reshape(1, -1),
        "rel_bias": rel_bias[l],
        "w_a_up": w_a_up[l].astype(BF16), "w_b_up": w_b_up[l].astype(BF16), "w_out": w_out[l].astype(BF16),
        "g_cross": g_cross[l], "w_cq": w_cq[l].astype(BF16), "w_co": w_co[l].astype(BF16),
        "g_ffn": g_ffn[l],
        "w_pq_t": w_pq[l].T.astype(BF16),
        "sub_keys": sub_keys[l].reshape(2 * PEER_HEADS, N_KEYS, PEER_HALF).astype(BF16),
        "peer_u": peer_u[l].astype(BF16),
        "peer_v_t": peer_v[l].T.astype(BF16),
        "g_final": g_final,
    }

    mk_flat, mv_flat = _norm_proj(mem_prompt.reshape(bsz_p * MEM_TOKENS, d), g_mem[l],
                                  [w_mk[l].astype(BF16), w_mv[l].astype(BF16)], 256)
    mk_p = mk_flat.reshape(bsz_p, MEM_TOKENS, d)
    mv_p = mv_flat.reshape(bsz_p, MEM_TOKENS, d)

    yp, sp = _layer(x_prompt, mk_p, mv_p, None, None, wts, 256)
    ys, sq = _layer(x_sample, cache_mem_k[l].reshape(bsz_s, MEM_TOKENS, d), cache_mem_v[l].reshape(bsz_s, MEM_TOKENS, d),
                    (cache_a_k[l], cache_a_v[l]), (state_b_conv[l], state_b_C[l], state_b_n[l], state_b_m[l]), wts, 256)

    ps = sp + (mk_p.reshape(bsz_p, MEM_TOKENS, X_HEADS, X_HEAD_DIM), mv_p.reshape(bsz_p, MEM_TOKENS, X_HEADS, X_HEAD_DIM))
    return (yp, ys) + tuple(a[None] for a in ps) + tuple(a[None] for a in sq)
```
